```python
import math
import jax, jax.numpy as jnp
from jax import lax
import numpy as np

D_MODEL = 2048
BATCH = 4
SEQ = 2048
DEPTH = 2
DEC_BATCH = 128
DEC_SEQ = 4
PAST_LEN = 8192
PAGE_SIZE = 128

N_AB = (DEPTH + 1) // 2
N_CD = DEPTH // 2
SB_HEADS = 8
SB_KV_HEADS = 4
SB_GROUP = SB_HEADS // SB_KV_HEADS
SB_HEAD_DIM = D_MODEL // 16
MLA_HEADS = 8
MLA_Q_LORA = D_MODEL // 4
MLA_KV_LORA = D_MODEL // 8
MLA_NOPE = 128
MLA_ROPE = 64
MLA_V = 128
MLA_SCALE = (MLA_NOPE + MLA_ROPE) ** -0.5
ROPE_THETA = 10000.0
Q_BLOCK = 128
GLA_HEADS = 4
GLA_DK = D_MODEL // 16
GLA_DV = D_MODEL // 8
GLA_GATE_RANK = 16
GLA_GATE_NORM = 16.0
HGRN_HEADS = 8
HGRN_EXPAND = 128
HGRN_DV = D_MODEL // 16
HGRN_DIM = HGRN_HEADS * HGRN_EXPAND
CHUNK = 64
N_EXPERTS = 32
TOP_K = 4
D_FF = D_MODEL
SWIGLU_LIMIT = 7.0
SWIGLU_ALPHA = 1.702
EPS = 1e-6

AB_SIZES = [SB_HEADS * SB_HEAD_DIM, SB_KV_HEADS * SB_HEAD_DIM, SB_KV_HEADS * SB_HEAD_DIM, MLA_Q_LORA, MLA_KV_LORA, MLA_ROPE]
AB_IN = sum(AB_SIZES)
AB_OUT = SB_HEADS * SB_HEAD_DIM + MLA_HEADS * MLA_V
CD_SIZES = [GLA_HEADS * GLA_DK, GLA_HEADS * GLA_DK, GLA_HEADS * GLA_DV, GLA_GATE_RANK, GLA_HEADS * GLA_DV, HGRN_DIM, HGRN_DIM, HGRN_HEADS * HGRN_DV, HGRN_HEADS * HGRN_DV]
CD_IN = sum(CD_SIZES)
CD_OUT = GLA_HEADS * GLA_DV + HGRN_HEADS * HGRN_DV

kernel_name = "hybrid_sb_mla_gla_hgrn2_moe_step"


def split_cols(a, sizes):
    return jnp.split(a, np.cumsum(sizes)[:-1].tolist(), axis=-1)


def rmsnorm(x, g):
    xf = x.astype(jnp.float32)
    y = xf * lax.rsqrt(jnp.mean(xf * xf, axis=-1, keepdims=True) + EPS)
    return (y * g.astype(jnp.float32)).astype(x.dtype)


def rope(x, pos):
    half = x.shape[-1] // 2
    inv = ROPE_THETA ** (-jnp.arange(half, dtype=jnp.float32) / half)
    ang = pos.astype(jnp.float32)[:, None] * inv
    ang = ang.reshape((ang.shape[0],) + (1,) * (x.ndim - 3) + (half,))
    cos, sin = jnp.cos(ang), jnp.sin(ang)
    xf = x.astype(jnp.float32)
    x1, x2 = xf[..., :half], xf[..., half:]
    return jnp.concatenate([x1 * cos - x2 * sin, x2 * cos + x1 * sin], axis=-1).astype(x.dtype)


def adaln(c, w, b):
    m = (jax.nn.silu(c) @ w + b)[:, None, :]
    return jnp.split(m, 6, axis=-1)


def gather_pages(cache, j, page_table):
    g = cache[j, page_table]
    return g.reshape((page_table.shape[0], page_table.shape[1] * PAGE_SIZE) + cache.shape[3:])


def ab_project(h, pos, w_in, g_q_a, w_q_b, g_kv_a, w_kv_b):
    B, T, _ = h.shape
    sb_q, sb_k, sb_v, q_a, c_kv, k_rope = split_cols(h @ w_in, AB_SIZES)
    q = (rmsnorm(q_a, g_q_a) @ w_q_b).reshape(B, T, MLA_HEADS, MLA_NOPE + MLA_ROPE)
    q_lat = jnp.einsum('bthn,chn->bthc', q[..., :MLA_NOPE], w_kv_b[..., :MLA_NOPE])
    q_rope = rope(q[..., MLA_NOPE:], pos)
    queries = (sb_q.reshape(B, T, SB_KV_HEADS, SB_GROUP, SB_HEAD_DIM), q_lat, q_rope)
    rows = (sb_k.reshape(B, T, SB_KV_HEADS, SB_HEAD_DIM), sb_v.reshape(B, T, SB_KV_HEADS, SB_HEAD_DIM),
            rmsnorm(c_kv, g_kv_a), rope(k_rope, pos))
    return queries, rows


def ab_attend(queries, q_pos, segments, w_kv_b):
    sb_q, q_lat, q_rope = queries
    B, Tq = sb_q.shape[:2]
    k_pos = jnp.concatenate([s[4] for s in segments])
    cuts = np.cumsum([s[0].shape[1] for s in segments])[:-1].tolist()
    z = jnp.concatenate([jnp.einsum('bqhgd,bshd->bhgqs', sb_q, s[0]) for s in segments], axis=-1).astype(jnp.float32) * SB_HEAD_DIM ** -0.5
    strict = k_pos[None, :] < q_pos[:, None]
    log_keep = jnp.where(strict, jax.nn.log_sigmoid(-z), 0.0)
    log_after = lax.cumsum(log_keep, axis=z.ndim - 1, reverse=True) - log_keep
    w_sb = jnp.where(strict, jnp.exp(jax.nn.log_sigmoid(z) + log_after), 0.0)
    o_sb = sum(jnp.einsum('bhgqs,bshd->bqhgd', w.astype(s[1].dtype), s[1])
               for w, s in zip(jnp.split(w_sb, cuts, axis=-1), segments))
    logits = jnp.concatenate([jnp.einsum('bqhc,bsc->bhqs', q_lat, s[2]) + jnp.einsum('bqhr,bsr->bhqs', q_rope, s[3])
                              for s in segments], axis=-1).astype(jnp.float32) * MLA_SCALE
    causal = k_pos[None, :] <= q_pos[:, None]
    p = jax.nn.softmax(jnp.where(causal, logits, -jnp.inf), axis=-1)
    o_lat = sum(jnp.einsum('bhqs,bsc->bqhc', w.astype(s[2].dtype), s[2])
                for w, s in zip(jnp.split(p, cuts, axis=-1), segments))
    o_mla = jnp.einsum('bqhc,chv->bqhv', o_lat, w_kv_b[..., MLA_NOPE:])
    return jnp.concatenate([o_sb.reshape(B, Tq, -1), o_mla.reshape(B, Tq, -1)], axis=-1)


def ab_prompt(queries, pos, rows, w_kv_b):
    B, T = queries[0].shape[:2]
    nb = T // Q_BLOCK

    def blocks(a):
        return a.reshape((B, nb, Q_BLOCK) + a.shape[2:]).swapaxes(0, 1)

    seg = [rows + (pos,)]
    out = lax.map(lambda blk: ab_attend(blk[:3], blk[3], seg, w_kv_b),
                  tuple(blocks(a) for a in queries) + (pos.reshape(nb, Q_BLOCK),))
    return out.swapaxes(0, 1).reshape(B, T, -1)


def gated_linear_scan(q, k, v, log_f, s0):
    B, T, H, DK = q.shape
    DV = v.shape[-1]
    c = CHUNK if T % CHUNK == 0 else T
    n = T // c
    tri = jnp.tril(jnp.ones((c, c), dtype=bool))[:, :, None]

    def to_chunks(a):
        return a.reshape(B, n, c, H, a.shape[-1]).transpose(1, 0, 3, 2, 4)

    def step(S, inp):
        qc, kc, vc, gc = (a.astype(jnp.float32) for a in inp)
        Sf = S.astype(jnp.float32)
        b = jnp.cumsum(gc, axis=2)
        decay = jnp.exp(jnp.where(tri, b[:, :, :, None, :] - b[:, :, None, :, :], -jnp.inf))
        att = jnp.einsum('bhid,bhjd,bhijd->bhij', qc, kc, decay)
        o = jnp.einsum('bhij,bhjv->bhiv', att, vc) + jnp.einsum('bhid,bhdv->bhiv', qc * jnp.exp(b), Sf)
        b_last = b[:, :, -1:, :]
        S_new = jnp.exp(b_last[:, :, 0, :, None]) * Sf + jnp.einsum('bhjd,bhjv->bhdv', kc * jnp.exp(b_last - b), vc)
        return S_new.astype(S.dtype), o.astype(v.dtype)

    s_fin, o = lax.scan(step, s0, (to_chunks(q), to_chunks(k), to_chunks(v), to_chunks(log_f)))
    return o.transpose(1, 0, 3, 2, 4).reshape(B, T, H, DV), s_fin


def cd_mixer(h, s_gla, s_hgrn, lb, w_in, w_gate, b_gate, g_gla, g_hgrn, w_out):
    B, T, _ = h.shape
    gq, gk, gv, ga, gr, hq, hf, hi, hr = split_cols(h @ w_in, CD_SIZES)
    heads = lambda a, nh: a.reshape(B, T, nh, -1)
    log_a = jax.nn.log_sigmoid((ga @ w_gate + b_gate).astype(jnp.float32)) / GLA_GATE_NORM
    o_c, s_gla_new = gated_linear_scan(heads(gq, GLA_HEADS) * GLA_DK ** -0.5, heads(gk, GLA_HEADS),
                                       heads(gv, GLA_HEADS), heads(log_a, GLA_HEADS), s_gla)
    o_c = rmsnorm(o_c, g_gla) * jax.nn.silu(heads(gr, GLA_HEADS))
    f = lb + (1.0 - lb) * jax.nn.sigmoid(hf.astype(jnp.float32))
    o_d, s_hgrn_new = gated_linear_scan(heads(jax.nn.silu(hq), HGRN_HEADS) * HGRN_EXPAND ** -0.5, heads(1.0 - f, HGRN_HEADS),
                                        heads(hi, HGRN_HEADS), heads(jnp.log(f), HGRN_HEADS), s_hgrn)
    o_d = rmsnorm(o_d, g_hgrn) * jax.nn.silu(heads(hr, HGRN_HEADS))
    o = jnp.concatenate([o_c.reshape(B, T, -1), o_d.reshape(B, T, -1)], axis=-1) @ w_out
    return o, (s_gla_new, s_hgrn_new)


def moe(h, l, w_router, b_router, w_gate_up, b_gate_up, w_down, b_down):
    B, T, D = h.shape
    xf = h.reshape(B * T, D)
    logits = (xf @ w_router[l] + b_router[l]).astype(jnp.float32)
    top_val, top_idx = lax.top_k(logits, TOP_K)
    combine = jnp.einsum('nk,nke->ne', jax.nn.softmax(top_val, axis=-1),
                         jax.nn.one_hot(top_idx, N_EXPERTS, dtype=jnp.float32))
    out = jnp.zeros((B * T, D), jnp.float32)
    for e in range(N_EXPERTS):
        gate, lin = jnp.split(xf @ w_gate_up[l, e] + b_gate_up[l, e], 2, axis=-1)
        gate = jnp.minimum(gate, SWIGLU_LIMIT)
        lin = jnp.clip(lin, -SWIGLU_LIMIT, SWIGLU_LIMIT)
        y = ((lin + 1.0) * gate * jax.nn.sigmoid(SWIGLU_ALPHA * gate)) @ w_down[l, e] + b_down[l, e]
        out = out + combine[:, e:e + 1] * y.astype(jnp.float32)
    return out.reshape(B, T, D).astype(h.dtype)


def residual_layer(x, c, mixer, w_ada_l, b_ada_l, g_norm_l, l, moe_w):
    sh1, sc1, g1, sh2, sc2, g2 = adaln(c, w_ada_l, b_ada_l)
    o, st = mixer(rmsnorm(x, g_norm_l[0]) * (1.0 + sc1) + sh1)
    x = x + g1 * o
    x = x + g2 * moe(rmsnorm(x, g_norm_l[1]) * (1.0 + sc2) + sh2, l, *moe_w)
    return x, st


def setup_inputs(seed: int = 0) -> dict:
    key = jax.random.key(seed)
    ks = iter(jax.random.split(key, 48))

    def nrm(shape, scale=1.0):
        return jax.random.normal(next(ks), shape, jnp.float32) * scale

    def gain(shape):
        return 1.0 + nrm(shape, 0.01)

    n_pages = PAST_LEN // PAGE_SIZE
    n_used = DEC_BATCH * n_pages
    n_pool = n_used + n_used // 4
    page_table = jax.random.permutation(next(ks), n_pool)[:n_used].reshape(DEC_BATCH, n_pages).astype(jnp.int32)
    D = D_MODEL
    return {
        "x_prompt": nrm((BATCH, SEQ, D)),
        "x_sample": nrm((DEC_BATCH, DEC_SEQ, D)),
        "cache_sb_k": nrm((N_AB, n_pool, PAGE_SIZE, SB_KV_HEADS, SB_HEAD_DIM)),
        "cache_sb_v": nrm((N_AB, n_pool, PAGE_SIZE, SB_KV_HEADS, SB_HEAD_DIM)),
        "cache_mla_latent": nrm((N_AB, n_pool, PAGE_SIZE, MLA_KV_LORA)),
        "cache_mla_krope": nrm((N_AB, n_pool, PAGE_SIZE, MLA_ROPE)),
        "state_gla": nrm((N_CD, DEC_BATCH, GLA_HEADS, GLA_DK, GLA_DV)),
        "state_hgrn": nrm((N_CD, DEC_BATCH, HGRN_HEADS, HGRN_EXPAND, HGRN_DV)),
        "page_table": page_table,
        "c_prompt": nrm((BATCH, D)),
        "c_sample": nrm((DEC_BATCH, D)),
        "w_ada": nrm((DEPTH, D, 6 * D), 0.5 * D ** -0.5),
        "b_ada": nrm((DEPTH, 6 * D), 0.01),
        "g_norm": gain((DEPTH, 2, D)),
        "w_in_ab": nrm((N_AB, D, AB_IN), D ** -0.5),
        "g_q_a": gain((N_AB, MLA_Q_LORA)),
        "w_q_b": nrm((N_AB, MLA_Q_LORA, MLA_HEADS * (MLA_NOPE + MLA_ROPE)), MLA_Q_LORA ** -0.5),
        "g_kv_a": gain((N_AB, MLA_KV_LORA)),
        "w_kv_b": nrm((N_AB, MLA_KV_LORA, MLA_HEADS, MLA_NOPE + MLA_V), MLA_KV_LORA ** -0.5),
        "w_out_ab": nrm((N_AB, AB_OUT, D), AB_OUT ** -0.5),
        "w_in_cd": nrm((N_CD, D, CD_IN), D ** -0.5),
        "w_gla_gate": nrm((N_CD, GLA_GATE_RANK, GLA_HEADS * GLA_DK), GLA_GATE_RANK ** -0.5),
        "b_gla_gate": nrm((N_CD, GLA_HEADS * GLA_DK), 0.01),
        "g_gla_out": gain((N_CD, GLA_DV)),
        "hgrn_lb": nrm((DEPTH, HGRN_DIM)),
        "g_hgrn_out": gain((N_CD, HGRN_DV)),
        "w_out_cd": nrm((N_CD, CD_OUT, D), CD_OUT ** -0.5),
        "w_router": nrm((DEPTH, D, N_EXPERTS), D ** -0.5),
        "b_router": nrm((DEPTH, N_EXPERTS), 0.01),
        "w_gate_up": nrm((DEPTH, N_EXPERTS, D, 2 * D_FF), D ** -0.5),
        "b_gate_up": nrm((DEPTH, N_EXPERTS, 2 * D_FF), 0.01),
        "w_down": nrm((DEPTH, N_EXPERTS, D_FF, D), D_FF ** -0.5),
        "b_down": nrm((DEPTH, N_EXPERTS, D), 0.01),
        "g_final": gain((D,)),
    }


def reference(x_prompt, x_sample, cache_sb_k, cache_sb_v, cache_mla_latent, cache_mla_krope, state_gla, state_hgrn,
              page_table, c_prompt, c_sample, w_ada, b_ada, g_norm, w_in_ab, g_q_a, w_q_b, g_kv_a, w_kv_b, w_out_ab,
              w_in_cd, w_gla_gate, b_gla_gate, g_gla_out, hgrn_lb, g_hgrn_out, w_out_cd, w_router, b_router,
              w_gate_up, b_gate_up, w_down, b_down, g_final):
    pos_p = jnp.arange(SEQ, dtype=jnp.int32)
    pos_s = PAST_LEN + jnp.arange(DEC_SEQ, dtype=jnp.int32)
    pos_past = jnp.arange(PAST_LEN, dtype=jnp.int32)
    lb_soft = jax.nn.softmax(hgrn_lb.astype(jnp.float32), axis=0)
    lower_bounds = jnp.cumsum(lb_soft, axis=0) - lb_soft[0]
    moe_w = (w_router, b_router, w_gate_up, b_gate_up, w_down, b_down)
    xp, xs = x_prompt, x_sample
    ab_p, ab_s, cd_p, cd_s = [], [], [], []
    for l in range(DEPTH):
        j = l // 2
        if l % 2 == 0:
            ab_w = (w_in_ab[j], g_q_a[j], w_q_b[j], g_kv_a[j], w_kv_b[j])

            def mix_prompt(h):
                queries, rows = ab_project(h, pos_p, *ab_w)
                return ab_prompt(queries, pos_p, rows, w_kv_b[j]) @ w_out_ab[j], rows

            def mix_sample(h):
                queries, rows = ab_project(h, pos_s, *ab_w)
                past = (gather_pages(cache_sb_k, j, page_table), gather_pages(cache_sb_v, j, page_table),
                        gather_pages(cache_mla_latent, j, page_table), gather_pages(cache_mla_krope, j, page_table), pos_past)
                o = ab_attend(queries, pos_s, [past, rows + (pos_s,)], w_kv_b[j])
                return o @ w_out_ab[j], rows

            xp, st_p = residual_layer(xp, c_prompt, mix_prompt, w_ada[l], b_ada[l], g_norm[l], l, moe_w)
            xs, st_s = residual_layer(xs, c_sample, mix_sample, w_ada[l], b_ada[l], g_norm[l], l, moe_w)
            ab_p.append(st_p)
            ab_s.append(st_s)
        else:
            lb = lower_bounds[l]
            cd_w = (w_in_cd[j], w_gla_gate[j], b_gla_gate[j], g_gla_out[j], g_hgrn_out[j], w_out_cd[j])
            s0_gla = jnp.zeros((BATCH, GLA_HEADS, GLA_DK, GLA_DV), x_prompt.dtype)
            s0_hgrn = jnp.zeros((BATCH, HGRN_HEADS, HGRN_EXPAND, HGRN_DV), x_prompt.dtype)
            xp, st_p = residual_layer(xp, c_prompt, lambda h: cd_mixer(h, s0_gla, s0_hgrn, lb, *cd_w),
                                      w_ada[l], b_ada[l], g_norm[l], l, moe_w)
            xs, st_s = residual_layer(xs, c_sample, lambda h: cd_mixer(h, state_gla[j], state_hgrn[j], lb, *cd_w),
                                      w_ada[l], b_ada[l], g_norm[l], l, moe_w)
            cd_p.append(st_p)
            cd_s.append(st_s)
    y_prompt = rmsnorm(xp, g_final)
    y_sample = rmsnorm(xs, g_final)
    sb_k_p = jnp.stack([r[0] for r in ab_p])
    sb_v_p = jnp.stack([r[1] for r in ab_p])
    lat_p = jnp.stack([r[2] for r in ab_p])
    krope_p = jnp.stack([r[3] for r in ab_p])
    gla_p = jnp.stack([r[0] for r in cd_p])
    hgrn_p = jnp.stack([r[1] for r in cd_p])
    sb_k_s = jnp.stack([r[0] for r in ab_s])
    sb_v_s = jnp.stack([r[1] for r in ab_s])
    lat_s = jnp.stack([r[2] for r in ab_s])
    krope_s = jnp.stack([r[3] for r in ab_s])
    gla_s = jnp.stack([r[0] for r in cd_s])
    hgrn_s = jnp.stack([r[1] for r in cd_s])
    return (y_prompt, y_sample, sb_k_p, sb_v_p, lat_p, krope_p, gla_p, hgrn_p,
            sb_k_s, sb_v_s, lat_s, krope_s, gla_s, hgrn_s)
```

```python
import functools
import math

import numpy as np
import jax
import jax.numpy as jnp
from jax import lax
from jax.experimental import pallas as pl
from jax.experimental.pallas import tpu as pltpu

F32 = jnp.float32
BF16 = jnp.bfloat16
I32 = jnp.int32
U32 = jnp.uint32
SDS = jax.ShapeDtypeStruct

D_MODEL = 2048
EPS = 1e-6
PAGE = 128
SB_HEADS, SB_KV, SB_GROUP, SB_DIM = 8, 4, 2, 128
MLA_HEADS, MLA_QL, MLA_KVL, MLA_NOPE, MLA_ROPE, MLA_V = 8, 512, 256, 128, 64, 128
MLA_SCALE = (MLA_NOPE + MLA_ROPE) ** -0.5
SB_SCALE = SB_DIM ** -0.5
ROPE_THETA = 10000.0
MLA_QW = MLA_KVL + 128
GLA_HEADS, GLA_DK, GLA_DV, GLA_RANK, GLA_NORM = 4, 128, 256, 16, 16.0
HGRN_HEADS, HGRN_DK, HGRN_DV = 8, 128, 128
SCAN_CHUNK = 64
SCAN_SUB = 16
N_EXPERTS, TOP_K = 32, 4
SWIGLU_LIMIT, SWIGLU_ALPHA = 7.0, 1.702
NEG = -1e30

TM = 512
MOE_ROWS = 1280
MOE_SUB = 256
MOE_F = 256
ROW_ALIGN = 8
MIB = 1 << 20


def _cp(sem, vmem_mib):
    return pltpu.CompilerParams(dimension_semantics=sem, vmem_limit_bytes=vmem_mib * MIB)


def _bf(x):
    return x.astype(BF16)


def _dot(a, b):
    return jnp.dot(a, b, preferred_element_type=F32)


def _dot_nt(a, b):
    return lax.dot_general(a, b, (((1,), (1,)), ((), ())), preferred_element_type=F32)


def _dot_tn(a, b):
    return lax.dot_general(a, b, (((0,), (0,)), ((), ())), preferred_element_type=F32)


def _sigmoid(x):
    return 1.0 / (1.0 + jnp.exp(-x))


def _silu(x):
    return x * _sigmoid(x)


def _softplus(x):
    return jnp.maximum(x, 0.0) + jnp.log(1.0 + jnp.exp(-jnp.abs(x)))


def _rms(x, g):
    return x * lax.rsqrt(jnp.mean(x * x, axis=-1, keepdims=True) + EPS) * g


def _split_bf16(x):
    hi = _bf(x)
    lo = _bf(x - hi.astype(F32))
    return hi, lo


class Tok:
    def __init__(self, batch, seq, dec_batch, dec_seq):
        self.batch, self.seq, self.dec_batch, self.dec_seq = batch, seq, dec_batch, dec_seq
        self.n_p = batch * seq
        self.n_s = dec_batch * dec_seq
        self.n = self.n_p + self.n_s
        assert seq % TM == 0 and self.n_s % TM == 0
        self.tiles_per_batch = seq // TM
        self.p_tiles = self.n_p // TM
        self.tiles = self.n // TM

    def mod_specs(self, k, width, col_of=None, grid_axis=0, ngrid=1):
        nblk = D_MODEL // width

        def pick(args):
            i = args[grid_axis]
            c = k * nblk + (col_of(*args) if col_of is not None else 0)
            return i, c

        def pmap(*args):
            i, c = pick(args)
            return jnp.minimum(i // self.tiles_per_batch, self.batch - 1), 0, c

        def smap(*args):
            i, c = pick(args)
            return jnp.maximum(i - self.p_tiles, 0), c

        return [pl.BlockSpec((None, 1, width), pmap), pl.BlockSpec((TM, width), smap)]


def _adaln_body(c_ref, w_ref, b_ref, o_ref):
    s = _bf(_silu(c_ref[...]))
    o_ref[...] = _dot(s, _bf(w_ref[...])) + b_ref[...]


def adaln_call(c_all, w_ada, b_ada):
    r = c_all.shape[0]
    nl, d, w6 = w_ada.shape
    tn = 1024
    return pl.pallas_call(
        _adaln_body,
        grid=(nl, w6 // tn),
        in_specs=[pl.BlockSpec((r, d), lambda l, n: (0, 0)),
                  pl.BlockSpec((None, d, tn), lambda l, n: (l, 0, n)),
                  pl.BlockSpec((None, 1, tn), lambda l, n: (l, 0, n))],
        out_specs=pl.BlockSpec((None, r, tn), lambda l, n: (l, 0, n)),
        out_shape=SDS((nl, r, w6), F32),
        compiler_params=_cp(("arbitrary", "arbitrary"), 40),
        name="adaln",
    )(c_all, w_ada, b_ada.reshape(nl, 1, w6))


def _normmod_body(x_ref, g_ref, scp_ref, scs_ref, shp_ref, shs_ref, o_ref, *, p_tiles):
    is_s = pl.program_id(0) >= p_tiles
    y = _rms(x_ref[...], g_ref[...])
    sc = jnp.where(is_s, scs_ref[...], scp_ref[...])
    sh = jnp.where(is_s, shs_ref[...], shp_ref[...])
    o_ref[...] = (y * (1.0 + sc) + sh).astype(o_ref.dtype)


def normmod_call(tok, x, g, mod_p, mod_s, k_shift, k_scale):
    d = D_MODEL
    return pl.pallas_call(
        functools.partial(_normmod_body, p_tiles=tok.p_tiles),
        grid=(tok.tiles,),
        in_specs=[pl.BlockSpec((TM, d), lambda i: (i, 0)), pl.BlockSpec((1, d), lambda i: (0, 0))]
        + tok.mod_specs(k_scale, d) + tok.mod_specs(k_shift, d),
        out_specs=pl.BlockSpec((TM, d), lambda i: (i, 0)),
        out_shape=SDS((tok.n, d), BF16),
        compiler_params=_cp(("arbitrary",), 56),
        name="normmod",
    )(x, g.reshape(1, d), mod_p, mod_s, mod_p, mod_s)


def _mm_body(a_ref, w_ref, o_ref, wbf_ref):
    @pl.when(pl.program_id(1) == 0)
    def _():
        wbf_ref[...] = _bf(w_ref[...])

    o_ref[...] = _dot(a_ref[...], wbf_ref[...])


def _mm_res_body(a_ref, w_ref, r_ref, gp_ref, gs_ref, o_ref, wbf_ref, *, p_tiles):
    @pl.when(pl.program_id(1) == 0)
    def _():
        wbf_ref[...] = _bf(w_ref[...])

    gate = jnp.where(pl.program_id(1) >= p_tiles, gs_ref[...], gp_ref[...])
    o_ref[...] = r_ref[...] + gate * _dot(a_ref[...], wbf_ref[...])


def matmul_call(a, w, tn, name):
    n, k = a.shape
    nout = w.shape[1]
    assert n % TM == 0 and nout % tn == 0
    return pl.pallas_call(
        _mm_body,
        grid=(nout // tn, n // TM),
        in_specs=[pl.BlockSpec((TM, k), lambda j, i: (i, 0)), pl.BlockSpec((k, tn), lambda j, i: (0, j))],
        out_specs=pl.BlockSpec((TM, tn), lambda j, i: (i, j)),
        out_shape=SDS((n, nout), F32),
        scratch_shapes=[pltpu.VMEM((k, tn), BF16)],
        compiler_params=_cp(("arbitrary", "arbitrary"), 48),
        name=name,
    )(a, w)


def matmul_res_call(tok, a, w, res, mod_p, mod_s, k_gate, name):
    n, k = a.shape
    nout = w.shape[1]
    tn = 1024
    return pl.pallas_call(
        functools.partial(_mm_res_body, p_tiles=tok.p_tiles),
        grid=(nout // tn, n // TM),
        in_specs=[pl.BlockSpec((TM, k), lambda j, i: (i, 0)), pl.BlockSpec((k, tn), lambda j, i: (0, j)),
                  pl.BlockSpec((TM, tn), lambda j, i: (i, j))]
        + tok.mod_specs(k_gate, tn, col_of=lambda j, i: j, grid_axis=1),
        out_specs=pl.BlockSpec((TM, tn), lambda j, i: (i, j)),
        out_shape=SDS((n, nout), F32),
        scratch_shapes=[pltpu.VMEM((k, tn), BF16)],
        compiler_params=_cp(("arbitrary", "arbitrary"), 48),
        name=name,
    )(a, w, res, mod_p, mod_s)


AB_TM = 256


def _ab_post_body(p_ref, cos_ref, sin_ref, gq_ref, gkv_ref, wq_ref, wk_ref,
                  sbq_ref, sbk_ref, sbv_ref, sbkb_ref, sbvb_ref, lat_ref, kr_ref, qm_ref, km_ref,
                  wqb_ref, wkb_ref):
    @pl.when(pl.program_id(0) == 0)
    def _():
        wqb_ref[...] = _bf(wq_ref[...])
        wkb_ref[...] = _bf(wk_ref[...])

    cos = cos_ref[...]
    sin = sin_ref[...]
    sbq_ref[...] = _bf(p_ref[:, 0:1024])
    sbk = p_ref[:, 1024:1536]
    sbv = p_ref[:, 1536:2048]
    sbk_ref[...] = sbk
    sbv_ref[...] = sbv
    sbkb_ref[...] = _bf(sbk)
    sbvb_ref[...] = _bf(sbv)
    lat = _rms(p_ref[:, 2560:2816], gkv_ref[...])
    lat_ref[...] = lat
    kr = p_ref[:, 2816:2944] * cos + p_ref[:, 2944:3072] * sin
    kr_ref[...] = kr[:, :MLA_ROPE]
    km_ref[:, 0:MLA_KVL] = _bf(lat)
    km_ref[:, MLA_KVL:MLA_QW] = _bf(kr)
    qn = _bf(_rms(p_ref[:, 2048:2560], gq_ref[...]))
    q2 = _dot(qn, wqb_ref[...])
    for h in range(MLA_HEADS):
        qlat = _dot(_bf(q2[:, h * 128:(h + 1) * 128]), wkb_ref[h])
        qr = q2[:, 1024 + h * 128:1024 + (h + 1) * 128] * cos + q2[:, 2048 + h * 128:2048 + (h + 1) * 128] * sin
        qm_ref[:, h * MLA_QW:h * MLA_QW + MLA_KVL] = _bf(qlat)
        qm_ref[:, h * MLA_QW + MLA_KVL:(h + 1) * MLA_QW] = _bf(qr)


def ab_post_call(proj, cos_t, sin_t, g_q_a, g_kv_a, wq_aug, wk_abs):
    n = proj.shape[0]
    t = AB_TM
    row = lambda w: pl.BlockSpec((t, w), lambda i: (i, 0))
    full = lambda shp: pl.BlockSpec(shp, lambda i: (0,) * len(shp))
    outs = [((n, 1024), BF16), ((n, 512), F32), ((n, 512), F32), ((n, 512), BF16), ((n, 512), BF16),
            ((n, MLA_KVL), F32), ((n, MLA_ROPE), F32), ((n, MLA_HEADS * MLA_QW), BF16), ((n, MLA_QW), BF16)]
    return pl.pallas_call(
        _ab_post_body,
        grid=(n // t,),
        in_specs=[row(3072), row(128), row(128), full((1, MLA_QL)), full((1, MLA_KVL)),
                  full((MLA_QL, 3072)), full((MLA_HEADS, MLA_NOPE, MLA_KVL))],
        out_specs=[row(s[1]) for s, _ in outs],
        out_shape=[SDS(s, dt) for s, dt in outs],
        scratch_shapes=[pltpu.VMEM((MLA_QL, 3072), BF16), pltpu.VMEM((MLA_HEADS, MLA_NOPE, MLA_KVL), BF16)],
        compiler_params=_cp(("arbitrary",), 48),
        name="ab_post",
    )(proj, cos_t, sin_t, g_q_a.reshape(1, -1), g_kv_a.reshape(1, -1), wq_aug, wk_abs)


def _sb_block(z, strict, u_bf, c):
    sp = _softplus(z)
    lk = -sp if strict is None else jnp.where(strict, -sp, 0.0)
    hi, lo = _split_bf16(lk)
    after = _dot(hi, u_bf) + _dot(lo, u_bf)
    w = jnp.exp(z - sp + after + c)
    if strict is not None:
        w = jnp.where(strict, w, 0.0)
    return w, c + jnp.sum(lk, axis=1, keepdims=True)


def _upper_mask(n):
    r = lax.broadcasted_iota(I32, (n, n), 0)
    c = lax.broadcasted_iota(I32, (n, n), 1)
    return jnp.where(r > c, 1.0, 0.0).astype(BF16)


SBP_TQ, SBP_TK = 256, 256


def _sb_prompt_body(q_ref, k_ref, v_ref, o_ref, acc_ref, c_ref):
    tq, tk = SBP_TQ, SBP_TK
    qi = pl.program_id(2)
    q = q_ref[...]
    q2 = jnp.concatenate([q[:, :SB_DIM], q[:, SB_DIM:]], axis=0)
    acc_ref[...] = jnp.zeros_like(acc_ref)
    c_ref[...] = jnp.zeros_like(c_ref)
    qpos = qi * tq + (lax.broadcasted_iota(I32, (2 * tq, tk), 0) & (tq - 1))
    col = lax.broadcasted_iota(I32, (2 * tq, tk), 1)
    u_bf = _upper_mask(tk)
    nkb = (qi * tq + tq - 1) // tk + 1

    def step(it, carry):
        ks = pl.multiple_of((nkb - 1 - it) * tk, tk)
        k = k_ref[pl.ds(ks, tk), :]
        v = v_ref[pl.ds(ks, tk), :]
        z = _dot_nt(q2, k) * SB_SCALE
        w, c_new = _sb_block(z, (ks + col) < qpos, u_bf, c_ref[...])
        acc_ref[...] += _dot(_bf(w), v)
        c_ref[...] = c_new
        return carry

    lax.fori_loop(0, nkb, step, 0)
    acc = acc_ref[...]
    o_ref[...] = _bf(jnp.concatenate([acc[:tq], acc[tq:]], axis=1))


def sb_prompt_call(tok, sbq, sbk_b, sbv_b):
    tq = SBP_TQ
    nq = tok.seq // tq
    return pl.pallas_call(
        _sb_prompt_body,
        grid=(tok.batch, SB_KV, nq),
        in_specs=[pl.BlockSpec((tq, SB_GROUP * SB_DIM), lambda b, h, i: (b * nq + i, h)),
                  pl.BlockSpec((tok.seq, SB_DIM), lambda b, h, i: (b, h)),
                  pl.BlockSpec((tok.seq, SB_DIM), lambda b, h, i: (b, h))],
        out_specs=pl.BlockSpec((tq, SB_GROUP * SB_DIM), lambda b, h, i: (b * nq + i, h)),
        out_shape=SDS((tok.n_p, SB_HEADS * SB_DIM), BF16),
        scratch_shapes=[pltpu.VMEM((2 * tq, SB_DIM), F32), pltpu.VMEM((2 * tq, 1), F32)],
        compiler_params=_cp(("arbitrary", "arbitrary", "arbitrary"), 40),
        name="sb_prompt",
    )(sbq, sbk_b, sbv_b)


MLAP_TQ, MLAP_TK = 128, 256


def _mla_prompt_body(q_ref, km_ref, wv_ref, o_ref, m_ref, l_ref, acc_ref):
    tq, tk = MLAP_TQ, MLAP_TK
    nh = MLA_HEADS
    qi = pl.program_id(1)
    q = q_ref[...]
    qs = jnp.concatenate([q[:, h * MLA_QW:(h + 1) * MLA_QW] for h in range(nh)], axis=0)
    m_ref[...] = jnp.full_like(m_ref, NEG)
    l_ref[...] = jnp.zeros_like(l_ref)
    acc_ref[...] = jnp.zeros_like(acc_ref)
    qpos = qi * tq + (lax.broadcasted_iota(I32, (nh * tq, tk), 0) & (tq - 1))
    col = lax.broadcasted_iota(I32, (nh * tq, tk), 1)
    nkb = (qi * tq + tq - 1) // tk + 1

    def step(kb, carry):
        ks = pl.multiple_of(kb * tk, tk)
        kblk = km_ref[pl.ds(ks, tk), :]
        s = _dot_nt(qs, kblk) * MLA_SCALE
        s = jnp.where((ks + col) <= qpos, s, NEG)
        m_old = m_ref[...]
        m_new = jnp.maximum(m_old, jnp.max(s, axis=1, keepdims=True))
        alpha = jnp.exp(m_old - m_new)
        p = jnp.exp(s - m_new)
        l_ref[...] = alpha * l_ref[...] + jnp.sum(p, axis=1, keepdims=True)
        acc_ref[...] = alpha * acc_ref[...] + _dot(_bf(p), kblk[:, :MLA_KVL])
        m_ref[...] = m_new
        return carry

    lax.fori_loop(0, nkb, step, 0)
    o_lat = acc_ref[...] / l_ref[...]
    for h in range(nh):
        o_ref[:, h * MLA_V:(h + 1) * MLA_V] = _bf(_dot(_bf(o_lat[h * tq:(h + 1) * tq]), _bf(wv_ref[h])))


def mla_prompt_call(tok, qm, km, wv):
    tq = MLAP_TQ
    nq = tok.seq // tq
    return pl.pallas_call(
        _mla_prompt_body,
        grid=(tok.batch, nq),
        in_specs=[pl.BlockSpec((tq, MLA_HEADS * MLA_QW), lambda b, i: (b * nq + i, 0)),
                  pl.BlockSpec((tok.seq, MLA_QW), lambda b, i: (b, 0)),
                  pl.BlockSpec((MLA_HEADS, MLA_KVL, MLA_V), lambda b, i: (0, 0, 0))],
        out_specs=pl.BlockSpec((tq, MLA_HEADS * MLA_V), lambda b, i: (b * nq + i, 0)),
        out_shape=SDS((tok.n_p, MLA_HEADS * MLA_V), BF16),
        scratch_shapes=[pltpu.VMEM((MLA_HEADS * tq, 1), F32), pltpu.VMEM((MLA_HEADS * tq, 1), F32),
                        pltpu.VMEM((MLA_HEADS * tq, MLA_KVL), F32)],
        compiler_params=_cp(("arbitrary", "arbitrary"), 40),
        name="mla_prompt",
    )(qm, km, wv)


DEC_PP = 2
DEC_QROWS = 8


def _decode_body(pt_ref, qsb_ref, qml_ref, nk_ref, nv_ref, nl_ref, nr_ref, wv_ref,
                 ck_hbm, cv_hbm, cl_hbm, cr_hbm, osb_ref, oml_ref,
                 kbuf, vbuf, lbuf, rbuf, sem, *, n_pages, layer):
    pp = DEC_PP
    nk = pp * PAGE
    b = pl.program_id(0)
    nb = pl.num_programs(0)
    steps = n_pages // pp
    srows = SB_KV * DEC_QROWS
    mrows = MLA_HEADS * DEC_QROWS

    def copies(bb, s, slot):
        out = []
        for j in range(pp):
            page = pt_ref[bb * n_pages + s * pp + j]
            out.append(pltpu.make_async_copy(ck_hbm.at[layer, page], kbuf.at[slot, j], sem.at[0, slot]))
            out.append(pltpu.make_async_copy(cv_hbm.at[layer, page], vbuf.at[slot, j], sem.at[1, slot]))
            out.append(pltpu.make_async_copy(cl_hbm.at[layer, page], lbuf.at[slot, j], sem.at[2, slot]))
            out.append(pltpu.make_async_copy(cr_hbm.at[layer, page], rbuf.at[slot, j], sem.at[3, slot]))
        return out

    @pl.when(b == 0)
    def _():
        for cp in copies(0, steps - 1, (steps - 1) % 2):
            cp.start()

    qsb = qsb_ref[...]
    qml = qml_ref[...]
    q_lat = qml[:, :MLA_KVL]
    q_rope = qml[:, MLA_KVL:MLA_KVL + MLA_ROPE]

    def sb_logits(kh):
        return jnp.concatenate([_dot_nt(qsb[h], kh(h)) for h in range(SB_KV)], axis=0) * SB_SCALE

    def sb_values(w, vh):
        wb = _bf(w)
        return jnp.concatenate([_dot(wb[h * DEC_QROWS:(h + 1) * DEC_QROWS], vh(h)) for h in range(SB_KV)], axis=0)

    tq_sb = lax.broadcasted_iota(I32, (srows, PAGE), 0) & (DEC_QROWS // SB_GROUP - 1)
    tq_ml = lax.broadcasted_iota(I32, (mrows, PAGE), 0) & (DEC_QROWS - 1)
    col_sb = lax.broadcasted_iota(I32, (srows, PAGE), 1)
    col_ml = lax.broadcasted_iota(I32, (mrows, PAGE), 1)
    nkk = _bf(nk_ref[...])
    nvv = _bf(nv_ref[...])
    z = sb_logits(lambda h: nkk[:, h * SB_DIM:(h + 1) * SB_DIM])
    w, c0 = _sb_block(z, col_sb < tq_sb, _upper_mask(PAGE), jnp.zeros((srows, 1), F32))
    acc0 = sb_values(w, lambda h: nvv[:, h * SB_DIM:(h + 1) * SB_DIM])

    nl = _bf(nl_ref[...])
    nr = _bf(nr_ref[...])
    s = (_dot_nt(q_lat, nl) + _dot_nt(q_rope, nr)) * MLA_SCALE
    s = jnp.where(col_ml <= tq_ml, s, NEG)
    m0 = jnp.max(s, axis=1, keepdims=True)
    p = jnp.exp(s - m0)
    l0 = jnp.sum(p, axis=1, keepdims=True)
    macc0 = _dot(_bf(p), nl)

    u_bf = _upper_mask(nk)

    def step(it, carry):
        c, acc, m, l, macc = carry
        s_idx = steps - 1 - it
        slot = s_idx % 2
        for cp in copies(b, s_idx, slot):
            cp.wait()

        @pl.when(s_idx > 0)
        def _():
            for cp in copies(b, s_idx - 1, 1 - slot):
                cp.start()

        @pl.when(jnp.logical_and(s_idx == 0, b + 1 < nb))
        def _():
            for cp in copies(b + 1, steps - 1, (steps - 1) % 2):
                cp.start()

        def kh(h):
            return _bf(kbuf[slot, :, :, h, :].reshape(nk, SB_DIM))

        def vh(h):
            return _bf(vbuf[slot, :, :, h, :].reshape(nk, SB_DIM))

        z = sb_logits(kh)
        w, c = _sb_block(z, None, u_bf, c)
        acc = acc + sb_values(w, vh)

        lat = _bf(lbuf[slot].reshape(nk, MLA_KVL))
        kr = _bf(rbuf[slot].reshape(nk, MLA_ROPE))
        s = (_dot_nt(q_lat, lat) + _dot_nt(q_rope, kr)) * MLA_SCALE
        m_new = jnp.maximum(m, jnp.max(s, axis=1, keepdims=True))
        alpha = jnp.exp(m - m_new)
        p = jnp.exp(s - m_new)
        l = alpha * l + jnp.sum(p, axis=1, keepdims=True)
        macc = alpha * macc + _dot(_bf(p), lat)
        return c, acc, m_new, l, macc

    c, acc, m, l, macc = lax.fori_loop(0, steps, step, (c0, acc0, m0, l0, macc0))
    for h in range(SB_KV):
        osb_ref[h] = _bf(acc[h * DEC_QROWS:(h + 1) * DEC_QROWS])
    o_lat = _bf(macc / l)
    for h in range(MLA_HEADS):
        oml_ref[:, h * MLA_V:(h + 1) * MLA_V] = _bf(_dot(o_lat[h * DEC_QROWS:(h + 1) * DEC_QROWS], _bf(wv_ref[h])))


def decode_call(page_table, qsb_d, qml_d, nk, nv, nl, nr, wv, cache_k, cache_v, cache_l, cache_r, layer):
    db, n_pages = page_table.shape
    assert n_pages % (2 * DEC_PP) == 0
    pp = DEC_PP
    blk = lambda shp: pl.BlockSpec((None,) + shp, lambda b, pt: (b,) + (0,) * len(shp))
    grid_spec = pltpu.PrefetchScalarGridSpec(
        num_scalar_prefetch=1,
        grid=(db,),
        in_specs=[blk((SB_KV, DEC_QROWS, SB_DIM)), blk((MLA_HEADS * DEC_QROWS, MLA_QW)),
                  blk((PAGE, SB_KV * SB_DIM)), blk((PAGE, SB_KV * SB_DIM)), blk((PAGE, MLA_KVL)), blk((PAGE, MLA_ROPE)),
                  pl.BlockSpec((MLA_HEADS, MLA_KVL, MLA_V), lambda b, pt: (0, 0, 0)),
                  pl.BlockSpec(memory_space=pl.ANY), pl.BlockSpec(memory_space=pl.ANY),
                  pl.BlockSpec(memory_space=pl.ANY), pl.BlockSpec(memory_space=pl.ANY)],
        out_specs=[blk((SB_KV, DEC_QROWS, SB_DIM)), blk((DEC_QROWS, MLA_HEADS * MLA_V))],
        scratch_shapes=[pltpu.VMEM((2, pp, PAGE, SB_KV, SB_DIM), F32), pltpu.VMEM((2, pp, PAGE, SB_KV, SB_DIM), F32),
                        pltpu.VMEM((2, pp, PAGE, MLA_KVL), F32), pltpu.VMEM((2, pp, PAGE, MLA_ROPE), F32),
                        pltpu.SemaphoreType.DMA((4, 2))],
    )
    return pl.pallas_call(
        functools.partial(_decode_body, n_pages=n_pages, layer=layer),
        grid_spec=grid_spec,
        out_shape=[SDS((db, SB_KV, DEC_QROWS, SB_DIM), BF16), SDS((db, DEC_QROWS, MLA_HEADS * MLA_V), BF16)],
        compiler_params=_cp(("arbitrary",), 40),
        name="decode_attn",
    )(page_table.reshape(-1), qsb_d, qml_d, nk, nv, nl, nr, wv, cache_k, cache_v, cache_l, cache_r)


def _scan_chunk(q, k, v, g, s, valid_rows=None):
    c, dk = q.shape
    sub = min(SCAN_SUB, c)
    nsub = c // sub
    row = lax.broadcasted_iota(I32, (c, c), 0)
    col = lax.broadcasted_iota(I32, (c, c), 1)
    if valid_rows is not None:
        live = lax.broadcasted_iota(I32, (c, dk), 0) < valid_rows
        g = jnp.where(live, g, 0.0)
        k = jnp.where(live, k, 0.0)
    tri = jnp.where(row >= col, 1.0, 0.0).astype(BF16)
    g_hi, g_lo = _split_bf16(g)
    bcum = _dot(tri, g_hi) + _dot(tri, g_lo)
    b_last = bcum[c - 1:c, :]
    rowk = lax.broadcasted_iota(I32, (c, dk), 0)
    lhs, rhs = [], []
    for j in range(nsub):
        b0 = bcum[j * sub:j * sub + 1, :]
        lhs.append(_bf(q * jnp.exp(jnp.minimum(bcum - b0, 0.0))))
        in_j = (rowk >= j * sub) & (rowk < (j + 1) * sub)
        rhs.append(_bf(jnp.where(in_j, k * jnp.exp(jnp.where(in_j, b0 - bcum, 0.0)), 0.0)))
    att = _dot_nt(jnp.concatenate(lhs, axis=1), jnp.concatenate(rhs, axis=1))
    att = jnp.where(row >= col, att, 0.0)
    v_bf = _bf(v)
    o = _dot(_bf(att), v_bf) + _dot(_bf(q * jnp.exp(bcum)), _bf(s))
    kd = _bf(k * jnp.exp(b_last - bcum))
    decay_col = jnp.transpose(jnp.broadcast_to(jnp.exp(b_last), (8, dk)))[:, 0:1]
    s_new = decay_col * s + _dot_tn(kd, v_bf)
    return o, s_new


def _gla_inputs(q_ref, k_ref, ga_ref, wg_ref, bg_ref):
    q = q_ref[...] * (GLA_DK ** -0.5)
    k = k_ref[...]
    x = _dot(_bf(ga_ref[...]), _bf(wg_ref[...])) + bg_ref[...]
    g = -_softplus(-x) / GLA_NORM
    return q, k, g


def _hgrn_inputs(q_ref, f_ref, lb_ref):
    q = _silu(q_ref[...]) * (HGRN_DK ** -0.5)
    lb = lb_ref[...]
    f = lb + (1.0 - lb) * _sigmoid(f_ref[...])
    return q, 1.0 - f, jnp.log(f)


def _scan_out(o, gain_ref, r_ref):
    return _bf(_rms(o, gain_ref[...]) * _silu(r_ref[...]))


SCAN_TB = 512


def _scan_prompt_body(*refs, kind):
    if kind == "gla":
        q_ref, k_ref, v_ref, r_ref, ga_ref, wg_ref, bg_ref, gain_ref, o_ref, st_ref, s_ref, obuf = refs
        q, k, g = _gla_inputs(q_ref, k_ref, ga_ref, wg_ref, bg_ref)
    else:
        q_ref, f_ref, v_ref, r_ref, lb_ref, gain_ref, o_ref, st_ref, s_ref, obuf = refs
        q, k, g = _hgrn_inputs(q_ref, f_ref, lb_ref)
    v = v_ref[...]
    tb = pl.program_id(2)

    @pl.when(tb == 0)
    def _():
        s_ref[...] = jnp.zeros_like(s_ref)

    cc = SCAN_CHUNK
    for ci in range(SCAN_TB // cc):
        sl = slice(ci * cc, (ci + 1) * cc)
        o, s_new = _scan_chunk(q[sl], k[sl], v[sl], g[sl], s_ref[...])
        s_ref[...] = s_new
        obuf[sl, :] = o
    o_ref[...] = _scan_out(obuf[...], gain_ref, r_ref)

    @pl.when(tb == pl.num_programs(2) - 1)
    def _():
        st_ref[...] = s_ref[...]


def scan_prompt_call(tok, proj, ga_proj, w_gate_pad, b_gate, gain, lb, kind):
    tb = SCAN_TB
    ntb = tok.seq // tb
    rows = lambda w, c0: pl.BlockSpec((tb, w), lambda b, h, t: (b * ntb + t, c0 // w + h))
    if kind == "gla":
        nh, dk, dv = GLA_HEADS, GLA_DK, GLA_DV
        in_specs = [rows(dk, 0), rows(dk, 512), rows(dv, 1024), rows(dv, 2048),
                    pl.BlockSpec((tb, 128), lambda b, h, t: (b * ntb + t, 0)),
                    pl.BlockSpec((128, dk), lambda b, h, t: (0, h)), pl.BlockSpec((1, dk), lambda b, h, t: (0, h)),
                    pl.BlockSpec((1, dv), lambda b, h, t: (0, 0))]
        args = (proj, proj, proj, proj, ga_proj, w_gate_pad, b_gate.reshape(1, -1), gain.reshape(1, -1))
    else:
        nh, dk, dv = HGRN_HEADS, HGRN_DK, HGRN_DV
        in_specs = [rows(dk, 3072), rows(dk, 4096), rows(dv, 5120), rows(dv, 6144),
                    pl.BlockSpec((1, dk), lambda b, h, t: (0, h)), pl.BlockSpec((1, dv), lambda b, h, t: (0, 0))]
        args = (proj, proj, proj, proj, lb.reshape(1, -1), gain.reshape(1, -1))
    return pl.pallas_call(
        functools.partial(_scan_prompt_body, kind=kind),
        grid=(tok.batch, nh, ntb),
        in_specs=in_specs,
        out_specs=[pl.BlockSpec((tb, dv), lambda b, h, t: (b * ntb + t, h)),
                   pl.BlockSpec((None, None, dk, dv), lambda b, h, t: (b, h, 0, 0))],
        out_shape=[SDS((tok.n_p, nh * dv), BF16), SDS((tok.batch, nh, dk, dv), F32)],
        scratch_shapes=[pltpu.VMEM((dk, dv), F32), pltpu.VMEM((tb, dv), F32)],
        compiler_params=_cp(("arbitrary", "arbitrary", "arbitrary"), 40),
        name="scan_prompt_" + kind,
    )(*args)


SCAN_SROWS = 8


def _scan_sample_body(*refs, kind, valid):
    if kind == "gla":
        p_ref, ga_ref, wg_ref, bg_ref, gain_ref, s0_ref, o_ref, st_ref = refs
        nh, dk, dv = GLA_HEADS, GLA_DK, GLA_DV
    else:
        p_ref, lb_ref, gain_ref, s0_ref, o_ref, st_ref = refs
        nh, dk, dv = HGRN_HEADS, HGRN_DK, HGRN_DV
    for h in range(nh):
        if kind == "gla":
            q = p_ref[:, h * dk:(h + 1) * dk] * (GLA_DK ** -0.5)
            k = p_ref[:, 512 + h * dk:512 + (h + 1) * dk]
            v = p_ref[:, 1024 + h * dv:1024 + (h + 1) * dv]
            r = p_ref[:, 2048 + h * dv:2048 + (h + 1) * dv]
            x = _dot(_bf(ga_ref[...]), _bf(wg_ref[:, h * dk:(h + 1) * dk])) + bg_ref[:, h * dk:(h + 1) * dk]
            g = -_softplus(-x) / GLA_NORM
        else:
            q = _silu(p_ref[:, 3072 + h * dk:3072 + (h + 1) * dk]) * (HGRN_DK ** -0.5)
            lb = lb_ref[:, h * dk:(h + 1) * dk]
            f = lb + (1.0 - lb) * _sigmoid(p_ref[:, 4096 + h * dk:4096 + (h + 1) * dk])
            k = 1.0 - f
            g = jnp.log(f)
            v = p_ref[:, 5120 + h * dv:5120 + (h + 1) * dv]
            r = p_ref[:, 6144 + h * dv:6144 + (h + 1) * dv]
        o, s_new = _scan_chunk(q, k, v, g, s0_ref[h], valid_rows=valid)
        st_ref[h] = s_new
        o_ref[:, h * dv:(h + 1) * dv] = _bf(_rms(o, gain_ref[...]) * _silu(r))


def scan_sample_call(proj_s, ga_s, w_gate_pad, b_gate, gain, lb, s0, kind, valid):
    db = proj_s.shape[0]
    r = SCAN_SROWS
    full = lambda shp: pl.BlockSpec(shp, lambda b: (0,) * len(shp))
    per_b = lambda shp: pl.BlockSpec((None,) + shp, lambda b: (b,) + (0,) * len(shp))
    if kind == "gla":
        nh, dk, dv = GLA_HEADS, GLA_DK, GLA_DV
        in_specs = [per_b((r, proj_s.shape[2])), per_b((r, 128)), full((128, nh * dk)), full((1, nh * dk)), full((1, dv)),
                    per_b((nh, dk, dv))]
        args = (proj_s, ga_s, w_gate_pad, b_gate.reshape(1, -1), gain.reshape(1, -1), s0)
    else:
        nh, dk, dv = HGRN_HEADS, HGRN_DK, HGRN_DV
        in_specs = [per_b((r, proj_s.shape[2])), full((1, nh * dk)), full((1, dv)), per_b((nh, dk, dv))]
        args = (proj_s, lb.reshape(1, -1), gain.reshape(1, -1), s0)
    return pl.pallas_call(
        functools.partial(_scan_sample_body, kind=kind, valid=valid),
        grid=(db,),
        in_specs=in_specs,
        out_specs=[per_b((r, nh * dv)), per_b((nh, dk, dv))],
        out_shape=[SDS((db, r, nh * dv), BF16), SDS((db, nh, dk, dv), F32)],
        compiler_params=_cp(("arbitrary",), 40),
        name="scan_sample_" + kind,
    )(*args)


def _moe_prep_body(x_ref, g_ref, scp_ref, scs_ref, shp_ref, shs_ref, wr_ref, br_ref,
                   hp_ref, sel_ref, idx_ref, wt_ref, *, p_tiles):
    is_s = pl.program_id(0) >= p_tiles
    y = _rms(x_ref[...], g_ref[...])
    sc = jnp.where(is_s, scs_ref[...], scp_ref[...])
    sh = jnp.where(is_s, shs_ref[...], shp_ref[...])
    h = y * (1.0 + sc) + sh
    h_hi, h_lo = _split_bf16(h)
    half = D_MODEL // 2
    hr = h_hi.astype(F32)
    lo_bits = lax.shift_right_logical(lax.bitcast_convert_type(hr[:, :half], U32), jnp.uint32(16))
    hi_bits = lax.bitcast_convert_type(hr[:, half:], U32)
    hp_ref[...] = hi_bits | lo_bits
    w_hi, w_lo = _split_bf16(wr_ref[...])
    logits = _dot(h_hi, w_hi) + _dot(h_lo, w_hi) + _dot(h_hi, w_lo) + br_ref[...]
    t, ne = logits.shape
    lane = lax.broadcasted_iota(I32, (t, ne), 1)
    lane4 = lax.broadcasted_iota(I32, (t, TOP_K), 1)
    vals = logits
    sel = jnp.zeros((t, ne), F32)
    idx4 = jnp.zeros((t, TOP_K), I32)
    e4 = jnp.zeros((t, TOP_K), F32)
    m0 = None
    for kk in range(TOP_K):
        m = jnp.max(vals, axis=1, keepdims=True)
        ik = jnp.min(jnp.where(vals == m, lane, ne), axis=1, keepdims=True)
        hot = lane == ik
        vals = jnp.where(hot, -jnp.inf, vals)
        sel = jnp.where(hot, 1.0, sel)
        if kk == 0:
            m0 = m
        idx4 = jnp.where(lane4 == kk, ik, idx4)
        e4 = jnp.where(lane4 == kk, jnp.exp(m - m0), e4)
    sel_ref[...] = sel.astype(sel_ref.dtype)
    idx_ref[...] = idx4
    wt_ref[...] = e4 / jnp.sum(e4, axis=1, keepdims=True)


def moe_prep_call(tok, x, g, mod_p, mod_s, w_router, b_router):
    d = D_MODEL
    row = lambda w: pl.BlockSpec((TM, w), lambda i: (i, 0))
    return pl.pallas_call(
        functools.partial(_moe_prep_body, p_tiles=tok.p_tiles),
        grid=(tok.tiles,),
        in_specs=[row(d), pl.BlockSpec((1, d), lambda i: (0, 0))] + tok.mod_specs(4, d) + tok.mod_specs(3, d)
        + [pl.BlockSpec((d, N_EXPERTS), lambda i: (0, 0)), pl.BlockSpec((1, N_EXPERTS), lambda i: (0, 0))],
        out_specs=[row(d // 2), row(N_EXPERTS), row(TOP_K), row(TOP_K)],
        out_shape=[SDS((tok.n, d // 2), U32), SDS((tok.n, N_EXPERTS), BF16), SDS((tok.n, TOP_K), I32),
                   SDS((tok.n, TOP_K), F32)],
        compiler_params=_cp(("arbitrary",), 48),
        name="moe_prep",
    )(x, g.reshape(1, d), mod_p, mod_s, mod_p, mod_s, w_router, b_router.reshape(1, -1))


RANK_TM = 256


def _moe_rank_body(sel_ref, rank_ref, cnt_ref):
    @pl.when(pl.program_id(0) == 0)
    def _():
        cnt_ref[...] = jnp.zeros_like(cnt_ref)

    sel = sel_ref[...]
    t = sel.shape[0]
    r = lax.broadcasted_iota(I32, (t, t), 0)
    c = lax.broadcasted_iota(I32, (t, t), 1)
    before = jnp.where(r > c, 1.0, 0.0).astype(BF16)
    base = cnt_ref[...]
    rank_ref[...] = base + _dot(before, sel)
    cnt_ref[...] = base + jnp.sum(sel.astype(F32), axis=0, keepdims=True)


def moe_rank_call(sel):
    n, ne = sel.shape
    t = RANK_TM
    return pl.pallas_call(
        _moe_rank_body,
        grid=(n // t,),
        in_specs=[pl.BlockSpec((t, ne), lambda i: (i, 0))],
        out_specs=[pl.BlockSpec((t, ne), lambda i: (i, 0)), pl.BlockSpec((1, ne), lambda i: (0, 0))],
        out_shape=[SDS((n, ne), F32), SDS((1, ne), F32)],
        compiler_params=_cp(("arbitrary",), 32),
        name="moe_rank",
    )(sel)


DISP_TM = 256


def _moe_dispatch_body(tail_ref, pos_ref, h_hbm, xs_hbm, zrows, sem):
    i = pl.program_id(0)
    t = DISP_TM

    @pl.when(i == 0)
    def _():
        zrows[...] = jnp.zeros_like(zrows)

        def tail_copy(e):
            return pltpu.make_async_copy(zrows, xs_hbm.at[pl.ds(pl.multiple_of(tail_ref[e], ROW_ALIGN), ROW_ALIGN)], sem)

        for e in range(N_EXPERTS):
            @pl.when(tail_ref[e] >= 0)
            def _(e=e):
                tail_copy(e).start()

        for e in range(N_EXPERTS):
            @pl.when(tail_ref[e] >= 0)
            def _(e=e):
                tail_copy(e).wait()

    def copy(tt, kk):
        return pltpu.make_async_copy(h_hbm.at[pl.ds(i * t + tt, 1)], xs_hbm.at[pl.ds(pos_ref[tt * TOP_K + kk], 1)], sem)

    def issue(tt, c):
        for kk in range(TOP_K):
            copy(tt, kk).start()
        return c

    def drain(tt, c):
        for kk in range(TOP_K):
            copy(tt, kk).wait()
        return c

    lax.fori_loop(0, t, issue, 0)
    lax.fori_loop(0, t, drain, 0)


def moe_dispatch_call(tail_rows, pos_flat, hp, n_rows):
    n, w = hp.shape
    t = DISP_TM
    grid_spec = pltpu.PrefetchScalarGridSpec(
        num_scalar_prefetch=1,
        grid=(n // t,),
        in_specs=[pl.BlockSpec((t * TOP_K,), lambda i, tl: (i,), memory_space=pltpu.SMEM),
                  pl.BlockSpec(memory_space=pl.ANY)],
        out_specs=pl.BlockSpec(memory_space=pl.ANY),
        scratch_shapes=[pltpu.VMEM((ROW_ALIGN, w), U32), pltpu.SemaphoreType.DMA(())],
    )
    return pl.pallas_call(
        _moe_dispatch_body,
        grid_spec=grid_spec,
        out_shape=SDS((n_rows, w), U32),
        compiler_params=_cp(("arbitrary",), 32),
        name="moe_dispatch",
    )(tail_rows, pos_flat, hp)


def _row_copies(src, src_off, dst, dst_off, nrows, sem, chunk, act):
    nfull = nrows // chunk

    def body(j, c):
        o = j * chunk
        act(pltpu.make_async_copy(src.at[pl.ds(pl.multiple_of(src_off + o, ROW_ALIGN), chunk)],
                                  dst.at[pl.ds(pl.multiple_of(dst_off + o, ROW_ALIGN), chunk)], sem))
        return c

    lax.fori_loop(0, nfull, body, 0)
    rem = nrows - nfull * chunk
    done = nfull * chunk
    bit = chunk // 2
    while bit >= ROW_ALIGN:
        take = (rem & bit) != 0

        @pl.when(take)
        def _(done=done, bit=bit):
            act(pltpu.make_async_copy(src.at[pl.ds(pl.multiple_of(src_off + done, ROW_ALIGN), bit)],
                                      dst.at[pl.ds(pl.multiple_of(dst_off + done, ROW_ALIGN), bit)], sem))

        done = done + jnp.where(take, bit, 0)
        bit //= 2


def _moe_expert_body(ie_ref, if_ref, start_ref, rows_ref,
                     xs_hbm, wg_ref, wl_ref, bg_ref, bl_ref, wd_ref, bd_ref, ys_hbm,
                     xin, xbf, yacc, wgb, wlb, wdb, sem):
    it = pl.program_id(0)
    f = pl.program_id(1)
    nf = pl.num_programs(1)
    rows = rows_ref[it]
    start = start_ref[it]
    half = D_MODEL // 2
    sub = MOE_SUB
    nsub = (rows + sub - 1) // sub

    @pl.when(jnp.logical_and(it == 0, f == 0))
    def _():
        xin[...] = jnp.zeros_like(xin)

    @pl.when(rows > 0)
    def _():
        @pl.when(f == 0)
        def _():
            _row_copies(xs_hbm, start, xin, 0, rows, sem.at[0], sub, lambda cp: cp.start())
            _row_copies(xs_hbm, start, xin, 0, rows, sem.at[0], sub, lambda cp: cp.wait())

            def unpack(r, c):
                o = pl.multiple_of(r * sub, sub)
                w = xin[pl.ds(o, sub), :]
                xbf[pl.ds(o, sub), 0:half] = _bf(lax.bitcast_convert_type(lax.shift_left(w, jnp.uint32(16)), F32))
                xbf[pl.ds(o, sub), half:] = _bf(lax.bitcast_convert_type(w & jnp.uint32(0xFFFF0000), F32))
                return c

            lax.fori_loop(0, nsub, unpack, 0)

        wgb[...] = _bf(wg_ref[...])
        wlb[...] = _bf(wl_ref[...])
        wdb[...] = _bf(wd_ref[...])

        def tile(r, c):
            o = pl.multiple_of(r * sub, sub)
            x = xbf[pl.ds(o, sub), :]
            gate = jnp.minimum(_dot(x, wgb[...]) + bg_ref[...], SWIGLU_LIMIT)
            lin = jnp.clip(_dot(x, wlb[...]) + bl_ref[...], -SWIGLU_LIMIT, SWIGLU_LIMIT)
            act = (lin + 1.0) * gate * _sigmoid(SWIGLU_ALPHA * gate)
            y = _dot(_bf(act), wdb[...])

            @pl.when(f == 0)
            def _():
                yacc[pl.ds(o, sub), :] = y + bd_ref[...]

            @pl.when(f > 0)
            def _():
                yacc[pl.ds(o, sub), :] += y

            return c

        lax.fori_loop(0, nsub, tile, 0)

        @pl.when(f == nf - 1)
        def _():
            _row_copies(yacc, 0, ys_hbm, start, rows, sem.at[1], sub, lambda cp: cp.start())
            _row_copies(yacc, 0, ys_hbm, start, rows, sem.at[1], sub, lambda cp: cp.wait())


def moe_expert_call(layer, item_e, item_f, item_start, item_rows, xs, w_gate_up, b_gate_up, w_down, b_down):
    nl, ne, d, ff2 = w_gate_up.shape
    ff = ff2 // 2
    nf = ff // MOE_F
    n_items = item_e.shape[0]
    n_rows = xs.shape[0]

    def fidx(i, f, ife):
        return jnp.where(ife[i] >= 0, f, nf - 1)

    grid_spec = pltpu.PrefetchScalarGridSpec(
        num_scalar_prefetch=4,
        grid=(n_items, nf),
        in_specs=[pl.BlockSpec(memory_space=pl.ANY),
                  pl.BlockSpec((None, None, d, MOE_F), lambda i, f, ie, ife, st, rw: (layer, ie[i], 0, fidx(i, f, ife))),
                  pl.BlockSpec((None, None, d, MOE_F), lambda i, f, ie, ife, st, rw: (layer, ie[i], 0, nf + fidx(i, f, ife))),
                  pl.BlockSpec((None, None, 1, MOE_F), lambda i, f, ie, ife, st, rw: (layer, ie[i], 0, fidx(i, f, ife))),
                  pl.BlockSpec((None, None, 1, MOE_F), lambda i, f, ie, ife, st, rw: (layer, ie[i], 0, nf + fidx(i, f, ife))),
                  pl.BlockSpec((None, None, MOE_F, d), lambda i, f, ie, ife, st, rw: (layer, ie[i], fidx(i, f, ife), 0)),
                  pl.BlockSpec((None, None, 1, d), lambda i, f, ie, ife, st, rw: (layer, ie[i], 0, 0))],
        out_specs=pl.BlockSpec(memory_space=pl.ANY),
        scratch_shapes=[pltpu.VMEM((MOE_ROWS, d // 2), U32), pltpu.VMEM((MOE_ROWS, d), BF16), pltpu.VMEM((MOE_ROWS, d), F32),
                        pltpu.VMEM((d, MOE_F), BF16), pltpu.VMEM((d, MOE_F), BF16), pltpu.VMEM((MOE_F, d), BF16),
                        pltpu.SemaphoreType.DMA((2,))],
    )
    return pl.pallas_call(
        _moe_expert_body,
        grid_spec=grid_spec,
        out_shape=SDS((n_rows, d), F32),
        compiler_params=_cp(("arbitrary", "arbitrary"), 56),
        name="moe_experts",
    )(item_e, item_f, item_start, item_rows, xs, w_gate_up, w_gate_up,
      b_gate_up.reshape(nl, ne, 1, ff2), b_gate_up.reshape(nl, ne, 1, ff2), w_down, b_down.reshape(nl, ne, 1, d))


COMB_TM = 256


def _moe_combine_body(pos_ref, ys_hbm, x_ref, wt_ref, gp_ref, gs_ref, gf_ref, o_ref, y_ref, buf, sem, *, p_tiles, tpb):
    i = pl.program_id(0)
    t = COMB_TM

    def copy(tt, kk):
        return pltpu.make_async_copy(ys_hbm.at[pl.ds(pos_ref[tt * TOP_K + kk], 1)], buf.at[kk, pl.ds(tt, 1)], sem)

    def issue(tt, c):
        for kk in range(TOP_K):
            copy(tt, kk).start()
        return c

    def drain(tt, c):
        for kk in range(TOP_K):
            copy(tt, kk).wait()
        return c

    lax.fori_loop(0, t, issue, 0)
    lax.fori_loop(0, t, drain, 0)
    wt = wt_ref[...]
    moe = wt[:, 0:1] * buf[0]
    for kk in range(1, TOP_K):
        moe = moe + wt[:, kk:kk + 1] * buf[kk]
    gate = jnp.where(i >= p_tiles * tpb, gs_ref[...], gp_ref[...])
    x = x_ref[...] + gate * moe
    o_ref[...] = x
    y_ref[...] = _rms(x, gf_ref[...])


def moe_combine_call(tok, pos_flat, ys, x, wt, mod_p, mod_s, g_final):
    d = D_MODEL
    t = COMB_TM
    tpb = TM // t
    row = lambda w: pl.BlockSpec((t, w), lambda i: (i, 0))

    def pmap(i):
        return jnp.minimum(i // (tok.tiles_per_batch * tpb), tok.batch - 1), 0, 5

    def smap(i):
        return jnp.maximum(i - tok.p_tiles * tpb, 0), 5

    return pl.pallas_call(
        functools.partial(_moe_combine_body, p_tiles=tok.p_tiles, tpb=tpb),
        grid=(tok.n // t,),
        in_specs=[pl.BlockSpec((t * TOP_K,), lambda i: (i,), memory_space=pltpu.SMEM),
                  pl.BlockSpec(memory_space=pl.ANY), row(d), row(TOP_K),
                  pl.BlockSpec((None, 1, d), pmap), pl.BlockSpec((t, d), smap),
                  pl.BlockSpec((1, d), lambda i: (0, 0))],
        out_specs=[row(d), row(d)],
        out_shape=[SDS((tok.n, d), F32), SDS((tok.n, d), F32)],
        scratch_shapes=[pltpu.VMEM((TOP_K, t, d), F32), pltpu.SemaphoreType.DMA(())],
        compiler_params=_cp(("arbitrary",), 48),
        name="moe_combine",
    )(pos_flat, ys, x, wt, mod_p, mod_s, g_final.reshape(1, d))


def moe_layer(tok, layer, x, g, mod_p, mod_s, w_router, b_router, w_gate_up, b_gate_up, w_down, b_down, g_final):
    hp, sel, idx4, wt = moe_prep_call(tok, x, g, mod_p, mod_s, w_router[layer], b_router[layer])
    rank, cnt = moe_rank_call(sel)
    real = cnt[0].astype(I32)
    counts = (real + ROW_ALIGN - 1) // ROW_ALIGN * ROW_ALIGN
    offs = jnp.cumsum(counts) - counts
    pos = jnp.take_along_axis(rank.astype(I32) + offs[None, :], idx4, axis=1)
    pos_flat = pos.reshape(-1)
    tail_rows = jnp.where(real > 0, offs + counts - ROW_ALIGN, -1).astype(I32)
    n_rows = tok.n * TOP_K + N_EXPERTS * ROW_ALIGN
    n_items = N_EXPERTS + n_rows // MOE_ROWS
    passes = (counts + MOE_ROWS - 1) // MOE_ROWS
    pend = jnp.cumsum(passes)
    total = pend[-1]
    ii = jnp.arange(n_items, dtype=I32)
    e_of = jnp.minimum(jnp.searchsorted(pend, ii, side="right").astype(I32), N_EXPERTS - 1)
    j_of = ii - (pend - passes)[e_of]
    valid = ii < total
    last_e = e_of[jnp.maximum(total - 1, 0)]
    item_e = jnp.where(valid, e_of, last_e)
    item_f = jnp.where(valid, 0, -1).astype(I32)
    item_start = jnp.where(valid, offs[e_of] + j_of * MOE_ROWS, 0).astype(I32)
    item_rows = jnp.where(valid, jnp.minimum(counts[e_of] - j_of * MOE_ROWS, MOE_ROWS), 0).astype(I32)
    xs = moe_dispatch_call(tail_rows, pos_flat, hp, n_rows)
    ys = moe_expert_call(layer, item_e, item_f, item_start, item_rows, xs, w_gate_up, b_gate_up, w_down, b_down)
    return moe_combine_call(tok, pos_flat, ys, x, wt, mod_p, mod_s, g_final)


def _rope_tables(pos):
    half = MLA_ROPE // 2
    inv = ROPE_THETA ** (-jnp.arange(half, dtype=F32) / half)
    ang = pos.astype(F32)[:, None] * inv
    cos, sin = jnp.cos(ang), jnp.sin(ang)
    z = jnp.zeros((pos.shape[0], 128 - MLA_ROPE), F32)
    return jnp.concatenate([cos, cos, z], axis=1), jnp.concatenate([-sin, sin, z], axis=1)


def _swap_halves(w):
    half = w.shape[-1] // 2
    return jnp.concatenate([w[..., half:], w[..., :half]], axis=-1)


def kernel(x_prompt, x_sample, cache_sb_k, cache_sb_v, cache_mla_latent, cache_mla_krope, state_gla, state_hgrn, page_table, c_prompt, c_sample, w_ada, b_ada, g_norm, w_in_ab, g_q_a, w_q_b, g_kv_a, w_kv_b, w_out_ab, w_in_cd, w_gla_gate, b_gla_gate, g_gla_out, hgrn_lb, g_hgrn_out, w_out_cd, w_router, b_router, w_gate_up, b_gate_up, w_down, b_down, g_final):
    batch, seq, d = x_prompt.shape
    db, ds, _ = x_sample.shape
    depth = w_ada.shape[0]
    past_len = page_table.shape[1] * PAGE
    tok = Tok(batch, seq, db, ds)
    assert d == D_MODEL and ds <= DEC_QROWS // SB_GROUP

    x = jnp.concatenate([x_prompt.reshape(tok.n_p, d), x_sample.reshape(tok.n_s, d)], axis=0)
    n_c = batch + db
    c_all = jnp.concatenate([c_prompt, c_sample, jnp.zeros((-n_c % 8, d), F32)], axis=0)
    mod = adaln_call(c_all, w_ada, b_ada)

    pos_all = jnp.concatenate([jnp.tile(jnp.arange(seq, dtype=I32), batch),
                               jnp.tile(past_len + jnp.arange(ds, dtype=I32), db)])
    cos_t, sin_t = _rope_tables(pos_all)
    lb_soft = jax.nn.softmax(hgrn_lb.astype(F32), axis=0)
    lower_bounds = jnp.cumsum(lb_soft, axis=0) - lb_soft[0]

    outs_ab, outs_cd = [], []
    y_final = None
    for l in range(depth):
        j = l // 2
        mod_p = mod[l, :batch].reshape(batch, 1, 6 * d)
        mod_s = jnp.repeat(mod[l, batch:n_c], ds, axis=0)
        h = normmod_call(tok, x, g_norm[l, 0], mod_p, mod_s, 0, 1)
        if l % 2 == 0:
            w_in = w_in_ab[j]
            zc = jnp.zeros((d, 128 - MLA_ROPE), F32)
            w_kr = w_in[:, 2816:2880]
            w_aug = jnp.concatenate([w_in[:, :2816], w_kr, zc, _swap_halves(w_kr), zc], axis=1)
            proj = matmul_call(h, w_aug, 1024, "in_proj_ab")
            wq = w_q_b[j].reshape(MLA_QL, MLA_HEADS, MLA_NOPE + MLA_ROPE)
            zq = jnp.zeros((MLA_QL, MLA_HEADS, 128 - MLA_ROPE), F32)
            wq_rope = wq[:, :, MLA_NOPE:]
            wq_aug = jnp.concatenate([wq[:, :, :MLA_NOPE].reshape(MLA_QL, -1),
                                      jnp.concatenate([wq_rope, zq], axis=2).reshape(MLA_QL, -1),
                                      jnp.concatenate([_swap_halves(wq_rope), zq], axis=2).reshape(MLA_QL, -1)], axis=1)
            wk_abs = jnp.transpose(w_kv_b[j][:, :, :MLA_NOPE], (1, 2, 0))
            wv = jnp.transpose(w_kv_b[j][:, :, MLA_NOPE:], (1, 0, 2))
            sbq, sbk, sbv, sbk_b, sbv_b, lat, krope, qm, km = ab_post_call(proj, cos_t, sin_t, g_q_a[j], g_kv_a[j], wq_aug, wk_abs)
            o_sb_p = sb_prompt_call(tok, sbq, sbk_b, sbv_b)
            o_ml_p = mla_prompt_call(tok, qm, km, wv)
            qsb_d = sbq[tok.n_p:].reshape(db, ds, SB_KV, SB_GROUP, SB_DIM).transpose(0, 2, 3, 1, 4)
            qsb_d = jnp.pad(qsb_d, ((0, 0), (0, 0), (0, 0), (0, DEC_QROWS // SB_GROUP - ds), (0, 0)))
            qsb_d = qsb_d.reshape(db, SB_KV, DEC_QROWS, SB_DIM)
            qml_d = qm[tok.n_p:].reshape(db, ds, MLA_HEADS, MLA_QW).transpose(0, 2, 1, 3)
            qml_d = jnp.pad(qml_d, ((0, 0), (0, 0), (0, DEC_QROWS - ds), (0, 0))).reshape(db, MLA_HEADS * DEC_QROWS, MLA_QW)
            padk = lambda a: jnp.pad(a[tok.n_p:].reshape(db, ds, -1), ((0, 0), (0, PAGE - ds), (0, 0)))
            o_sb_s, o_ml_s = decode_call(page_table, qsb_d, qml_d, padk(sbk), padk(sbv), padk(lat), padk(krope), wv,
                                         cache_sb_k, cache_sb_v, cache_mla_latent, cache_mla_krope, j)
            hs = DEC_QROWS // SB_GROUP
            o_sb_s = o_sb_s.reshape(db, SB_KV, SB_GROUP, hs, SB_DIM)[:, :, :, :ds].transpose(0, 3, 1, 2, 4).reshape(tok.n_s, -1)
            o_ml_s = o_ml_s[:, :ds].reshape(tok.n_s, -1)
            o = jnp.concatenate([jnp.concatenate([o_sb_p, o_ml_p], axis=1), jnp.concatenate([o_sb_s, o_ml_s], axis=1)], axis=0)
            x = matmul_res_call(tok, o, w_out_ab[j], x, mod_p, mod_s, 2, "out_proj_ab")
            outs_ab.append((sbk, sbv, lat, krope))
        else:
            w_in = w_in_cd[j]
            w_main = jnp.concatenate([w_in[:, :2048], w_in[:, 2064:]], axis=1)
            w_ga = jnp.concatenate([w_in[:, 2048:2064], jnp.zeros((d, 128 - GLA_RANK), F32)], axis=1)
            proj = matmul_call(h, w_main, 1024, "in_proj_cd")
            ga = matmul_call(h, w_ga, 128, "in_proj_gate")
            w_gate_pad = jnp.concatenate([w_gla_gate[j], jnp.zeros((128 - GLA_RANK, GLA_HEADS * GLA_DK), F32)], axis=0)
            lb = lower_bounds[l]
            o_gla_p, st_gla_p = scan_prompt_call(tok, proj, ga, w_gate_pad, b_gla_gate[j], g_gla_out[j], lb, "gla")
            o_hgrn_p, st_hgrn_p = scan_prompt_call(tok, proj, ga, w_gate_pad, b_gla_gate[j], g_hgrn_out[j], lb, "hgrn")
            pad_t = lambda a: jnp.pad(a[tok.n_p:].reshape(db, ds, -1), ((0, 0), (0, SCAN_SROWS - ds), (0, 0)))
            proj_s, ga_s = pad_t(proj), pad_t(ga)
            o_gla_s, st_gla_s = scan_sample_call(proj_s, ga_s, w_gate_pad, b_gla_gate[j], g_gla_out[j], lb, state_gla[j], "gla", ds)
            o_hgrn_s, st_hgrn_s = scan_sample_call(proj_s, ga_s, w_gate_pad, b_gla_gate[j], g_hgrn_out[j], lb, state_hgrn[j], "hgrn", ds)
            o_s = jnp.concatenate([o_gla_s[:, :ds].reshape(tok.n_s, -1), o_hgrn_s[:, :ds].reshape(tok.n_s, -1)], axis=1)
            o = jnp.concatenate([jnp.concatenate([o_gla_p, o_hgrn_p], axis=1), o_s], axis=0)
            x = matmul_res_call(tok, o, w_out_cd[j], x, mod_p, mod_s, 2, "out_proj_cd")
            outs_cd.append((st_gla_p, st_hgrn_p, st_gla_s, st_hgrn_s))
        x, y_final = moe_layer(tok, l, x, g_norm[l, 1], mod_p, mod_s, w_router, b_router, w_gate_up, b_gate_up,
                               w_down, b_down, g_final)

    y_prompt = y_final[:tok.n_p].reshape(batch, seq, d)
    y_sample = y_final[tok.n_p:].reshape(db, ds, d)

    def grp(a, s, shape):
        return a[s].reshape(shape)

    sp, ss = slice(0, tok.n_p), slice(tok.n_p, tok.n)
    sb_k_p = jnp.stack([grp(r[0], sp, (batch, seq, SB_KV, SB_DIM)) for r in outs_ab])
    sb_v_p = jnp.stack([grp(r[1], sp, (batch, seq, SB_KV, SB_DIM)) for r in outs_ab])
    lat_p = jnp.stack([grp(r[2], sp, (batch, seq, MLA_KVL)) for r in outs_ab])
    krope_p = jnp.stack([grp(r[3], sp, (batch, seq, MLA_ROPE)) for r in outs_ab])
    sb_k_s = jnp.stack([grp(r[0], ss, (db, ds, SB_KV, SB_DIM)) for r in outs_ab])
    sb_v_s = jnp.stack([grp(r[1], ss, (db, ds, SB_KV, SB_DIM)) for r in outs_ab])
    lat_s = jnp.stack([grp(r[2], ss, (db, ds, MLA_KVL)) for r in outs_ab])
    krope_s = jnp.stack([grp(r[3], ss, (db, ds, MLA_ROPE)) for r in outs_ab])
    gla_p = jnp.stack([r[0] for r in outs_cd])
    hgrn_p = jnp.stack([r[1] for r in outs_cd])
    gla_s = jnp.stack([r[2] for r in outs_cd])
    hgrn_s = jnp.stack([r[3] for r in outs_cd])
    return (y_prompt, y_sample, sb_k_p, sb_v_p, lat_p, krope_p, gla_p, hgrn_p,
            sb_k_s, sb_v_s, lat_s, krope_s, gla_s, hgrn_s)
```

```python
import functools
import math

import numpy as np
import jax
import jax.numpy as jnp
from jax import lax
from jax.experimental import pallas as pl
from jax.experimental.pallas import tpu as pltpu

F32 = jnp.float32
BF16 = jnp.bfloat16
I32 = jnp.int32
U32 = jnp.uint32
SDS = jax.ShapeDtypeStruct

D_MODEL = 2048
EPS = 1e-6
PAGE = 128
SB_HEADS, SB_KV, SB_GROUP, SB_DIM = 8, 4, 2, 128
MLA_HEADS, MLA_QL, MLA_KVL, MLA_NOPE, MLA_ROPE, MLA_V = 8, 512, 256, 128, 64, 128
MLA_SCALE = (MLA_NOPE + MLA_ROPE) ** -0.5
SB_SCALE = SB_DIM ** -0.5
ROPE_THETA = 10000.0
MLA_QW = MLA_KVL + 128
GLA_HEADS, GLA_DK, GLA_DV, GLA_RANK, GLA_NORM = 4, 128, 256, 16, 16.0
HGRN_HEADS, HGRN_DK, HGRN_DV = 8, 128, 128
SCAN_CHUNK = 64
SCAN_SUB = 16
N_EXPERTS, TOP_K = 32, 4
SWIGLU_LIMIT, SWIGLU_ALPHA = 7.0, 1.702
NEG = -1e30

TM = 512
MOE_ROWS = 1280
MOE_SUB = 256
MOE_F = 512
MOE_VMEM_MIB = 58
ROW_ALIGN = 8
MIB = 1 << 20


def _cp(sem, vmem_mib):
    return pltpu.CompilerParams(dimension_semantics=sem, vmem_limit_bytes=vmem_mib * MIB)


def _bf(x):
    return x.astype(BF16)


def _dot(a, b):
    return jnp.dot(a, b, preferred_element_type=F32)


def _dot_nt(a, b):
    return lax.dot_general(a, b, (((1,), (1,)), ((), ())), preferred_element_type=F32)


def _dot_tn(a, b):
    return lax.dot_general(a, b, (((0,), (0,)), ((), ())), preferred_element_type=F32)


def _sigmoid(x):
    return 1.0 / (1.0 + jnp.exp(-x))


def _silu(x):
    return x * _sigmoid(x)


def _softplus(x):
    return jnp.maximum(x, 0.0) + jnp.log(1.0 + jnp.exp(-jnp.abs(x)))


def _rms(x, g):
    return x * lax.rsqrt(jnp.mean(x * x, axis=-1, keepdims=True) + EPS) * g


def _split_bf16(x):
    hi = _bf(x)
    lo = _bf(x - hi.astype(F32))
    return hi, lo


class Tok:
    def __init__(self, batch, seq, dec_batch, dec_seq):
        self.batch, self.seq, self.dec_batch, self.dec_seq = batch, seq, dec_batch, dec_seq
        self.n_p = batch * seq
        self.n_s = dec_batch * dec_seq
        self.n = self.n_p + self.n_s
        assert seq % TM == 0 and self.n_s % TM == 0
        self.tiles_per_batch = seq // TM
        self.p_tiles = self.n_p // TM
        self.tiles = self.n // TM

    def mod_specs(self, k, width, col_of=None, grid_axis=0, ngrid=1):
        nblk = D_MODEL // width

        def pick(args):
            i = args[grid_axis]
            c = k * nblk + (col_of(*args) if col_of is not None else 0)
            return i, c

        def pmap(*args):
            i, c = pick(args)
            return jnp.minimum(i // self.tiles_per_batch, self.batch - 1), 0, c

        def smap(*args):
            i, c = pick(args)
            return jnp.maximum(i - self.p_tiles, 0), c

        return [pl.BlockSpec((None, 1, width), pmap), pl.BlockSpec((TM, width), smap)]


def _adaln_body(c_ref, w_ref, b_ref, o_ref):
    s = _bf(_silu(c_ref[...]))
    o_ref[...] = _dot(s, _bf(w_ref[...])) + b_ref[...]


def adaln_call(c_all, w_ada, b_ada):
    r = c_all.shape[0]
    nl, d, w6 = w_ada.shape
    tn = 1024
    return pl.pallas_call(
        _adaln_body,
        grid=(nl, w6 // tn),
        in_specs=[pl.BlockSpec((r, d), lambda l, n: (0, 0)),
                  pl.BlockSpec((None, d, tn), lambda l, n: (l, 0, n)),
                  pl.BlockSpec((None, 1, tn), lambda l, n: (l, 0, n))],
        out_specs=pl.BlockSpec((None, r, tn), lambda l, n: (l, 0, n)),
        out_shape=SDS((nl, r, w6), F32),
        compiler_params=_cp(("arbitrary", "arbitrary"), 40),
        name="adaln",
    )(c_all, w_ada, b_ada.reshape(nl, 1, w6))


def _normmod_body(x_ref, g_ref, scp_ref, scs_ref, shp_ref, shs_ref, o_ref, *, p_tiles):
    is_s = pl.program_id(0) >= p_tiles
    y = _rms(x_ref[...], g_ref[...])
    sc = jnp.where(is_s, scs_ref[...], scp_ref[...])
    sh = jnp.where(is_s, shs_ref[...], shp_ref[...])
    o_ref[...] = (y * (1.0 + sc) + sh).astype(o_ref.dtype)


def normmod_call(tok, x, g, mod_p, mod_s, k_shift, k_scale):
    d = D_MODEL
    return pl.pallas_call(
        functools.partial(_normmod_body, p_tiles=tok.p_tiles),
        grid=(tok.tiles,),
        in_specs=[pl.BlockSpec((TM, d), lambda i: (i, 0)), pl.BlockSpec((1, d), lambda i: (0, 0))]
        + tok.mod_specs(k_scale, d) + tok.mod_specs(k_shift, d),
        out_specs=pl.BlockSpec((TM, d), lambda i: (i, 0)),
        out_shape=SDS((tok.n, d), BF16),
        compiler_params=_cp(("arbitrary",), 56),
        name="normmod",
    )(x, g.reshape(1, d), mod_p, mod_s, mod_p, mod_s)


def _mm_body(a_ref, w_ref, o_ref, wbf_ref):
    @pl.when(pl.program_id(1) == 0)
    def _():
        wbf_ref[...] = _bf(w_ref[...])

    o_ref[...] = _dot(a_ref[...], wbf_ref[...])


def _mm_res_body(a_ref, w_ref, r_ref, gp_ref, gs_ref, o_ref, wbf_ref, *, p_tiles):
    @pl.when(pl.program_id(1) == 0)
    def _():
        wbf_ref[...] = _bf(w_ref[...])

    gate = jnp.where(pl.program_id(1) >= p_tiles, gs_ref[...], gp_ref[...])
    o_ref[...] = r_ref[...] + gate * _dot(a_ref[...], wbf_ref[...])


def matmul_call(a, w, tn, name):
    n, k = a.shape
    nout = w.shape[1]
    assert n % TM == 0 and nout % tn == 0
    return pl.pallas_call(
        _mm_body,
        grid=(nout // tn, n // TM),
        in_specs=[pl.BlockSpec((TM, k), lambda j, i: (i, 0)), pl.BlockSpec((k, tn), lambda j, i: (0, j))],
        out_specs=pl.BlockSpec((TM, tn), lambda j, i: (i, j)),
        out_shape=SDS((n, nout), F32),
        scratch_shapes=[pltpu.VMEM((k, tn), BF16)],
        compiler_params=_cp(("arbitrary", "arbitrary"), 48),
        name=name,
    )(a, w)


def matmul_res_call(tok, a, w, res, mod_p, mod_s, k_gate, name):
    n, k = a.shape
    nout = w.shape[1]
    tn = 1024
    return pl.pallas_call(
        functools.partial(_mm_res_body, p_tiles=tok.p_tiles),
        grid=(nout // tn, n // TM),
        in_specs=[pl.BlockSpec((TM, k), lambda j, i: (i, 0)), pl.BlockSpec((k, tn), lambda j, i: (0, j)),
                  pl.BlockSpec((TM, tn), lambda j, i: (i, j))]
        + tok.mod_specs(k_gate, tn, col_of=lambda j, i: j, grid_axis=1),
        out_specs=pl.BlockSpec((TM, tn), lambda j, i: (i, j)),
        out_shape=SDS((n, nout), F32),
        scratch_shapes=[pltpu.VMEM((k, tn), BF16)],
        compiler_params=_cp(("arbitrary", "arbitrary"), 48),
        name=name,
    )(a, w, res, mod_p, mod_s)


AB_TM = 256


def _ab_post_body(p_ref, cos_ref, sin_ref, gq_ref, gkv_ref, wq_ref, wk_ref,
                  sbq_ref, sbk_ref, sbv_ref, sbkb_ref, sbvb_ref, lat_ref, kr_ref, qm_ref, km_ref,
                  wqb_ref, wkb_ref):
    @pl.when(pl.program_id(0) == 0)
    def _():
        wqb_ref[...] = _bf(wq_ref[...])
        wkb_ref[...] = _bf(wk_ref[...])

    cos = cos_ref[...]
    sin = sin_ref[...]
    sbq_ref[...] = _bf(p_ref[:, 0:1024])
    sbk = p_ref[:, 1024:1536]
    sbv = p_ref[:, 1536:2048]
    sbk_ref[...] = sbk
    sbv_ref[...] = sbv
    sbkb_ref[...] = _bf(sbk)
    sbvb_ref[...] = _bf(sbv)
    lat = _rms(p_ref[:, 2560:2816], gkv_ref[...])
    lat_ref[...] = lat
    kr = p_ref[:, 2816:2944] * cos + p_ref[:, 2944:3072] * sin
    kr_ref[...] = kr[:, :MLA_ROPE]
    km_ref[:, 0:MLA_KVL] = _bf(lat)
    km_ref[:, MLA_KVL:MLA_QW] = _bf(kr)
    qn = _bf(_rms(p_ref[:, 2048:2560], gq_ref[...]))
    q2 = _dot(qn, wqb_ref[...])
    for h in range(MLA_HEADS):
        qlat = _dot(_bf(q2[:, h * 128:(h + 1) * 128]), wkb_ref[h])
        qr = q2[:, 1024 + h * 128:1024 + (h + 1) * 128] * cos + q2[:, 2048 + h * 128:2048 + (h + 1) * 128] * sin
        qm_ref[:, h * MLA_QW:h * MLA_QW + MLA_KVL] = _bf(qlat)
        qm_ref[:, h * MLA_QW + MLA_KVL:(h + 1) * MLA_QW] = _bf(qr)


def ab_post_call(proj, cos_t, sin_t, g_q_a, g_kv_a, wq_aug, wk_abs):
    n = proj.shape[0]
    t = AB_TM
    row = lambda w: pl.BlockSpec((t, w), lambda i: (i, 0))
    full = lambda shp: pl.BlockSpec(shp, lambda i: (0,) * len(shp))
    outs = [((n, 1024), BF16), ((n, 512), F32), ((n, 512), F32), ((n, 512), BF16), ((n, 512), BF16),
            ((n, MLA_KVL), F32), ((n, MLA_ROPE), F32), ((n, MLA_HEADS * MLA_QW), BF16), ((n, MLA_QW), BF16)]
    return pl.pallas_call(
        _ab_post_body,
        grid=(n // t,),
        in_specs=[row(3072), row(128), row(128), full((1, MLA_QL)), full((1, MLA_KVL)),
                  full((MLA_QL, 3072)), full((MLA_HEADS, MLA_NOPE, MLA_KVL))],
        out_specs=[row(s[1]) for s, _ in outs],
        out_shape=[SDS(s, dt) for s, dt in outs],
        scratch_shapes=[pltpu.VMEM((MLA_QL, 3072), BF16), pltpu.VMEM((MLA_HEADS, MLA_NOPE, MLA_KVL), BF16)],
        compiler_params=_cp(("arbitrary",), 48),
        name="ab_post",
    )(proj, cos_t, sin_t, g_q_a.reshape(1, -1), g_kv_a.reshape(1, -1), wq_aug, wk_abs)


def _sb_block(z, strict, u_bf, c):
    sp = _softplus(z)
    lk = -sp if strict is None else jnp.where(strict, -sp, 0.0)
    hi, lo = _split_bf16(lk)
    after = _dot(hi, u_bf) + _dot(lo, u_bf)
    w = jnp.exp(z - sp + after + c)
    if strict is not None:
        w = jnp.where(strict, w, 0.0)
    return w, c + jnp.sum(lk, axis=1, keepdims=True)


def _upper_mask(n):
    r = lax.broadcasted_iota(I32, (n, n), 0)
    c = lax.broadcasted_iota(I32, (n, n), 1)
    return jnp.where(r > c, 1.0, 0.0).astype(BF16)


SBP_TQ, SBP_TK = 256, 256


def _sb_prompt_body(q_ref, k_ref, v_ref, o_ref, acc_ref, c_ref):
    tq, tk = SBP_TQ, SBP_TK
    qi = pl.program_id(2)
    q = q_ref[...]
    q2 = jnp.concatenate([q[:, :SB_DIM], q[:, SB_DIM:]], axis=0)
    acc_ref[...] = jnp.zeros_like(acc_ref)
    c_ref[...] = jnp.zeros_like(c_ref)
    qpos = qi * tq + (lax.broadcasted_iota(I32, (2 * tq, tk), 0) & (tq - 1))
    col = lax.broadcasted_iota(I32, (2 * tq, tk), 1)
    u_bf = _upper_mask(tk)
    nkb = (qi * tq + tq - 1) // tk + 1
    nfull = (qi * tq) // tk

    def block(kb, masked):
        ks = pl.multiple_of(kb * tk, tk)
        k = k_ref[pl.ds(ks, tk), :]
        v = v_ref[pl.ds(ks, tk), :]
        z = _dot_nt(q2, k) * SB_SCALE
        w, c_new = _sb_block(z, ((ks + col) < qpos) if masked else None, u_bf, c_ref[...])
        acc_ref[...] += _dot(_bf(w), v)
        c_ref[...] = c_new

    def masked_step(it, carry):
        block(nkb - 1 - it, True)
        return carry

    def full_step(it, carry):
        block(nfull - 1 - it, False)
        return carry

    lax.fori_loop(0, nkb - nfull, masked_step, 0)
    lax.fori_loop(0, nfull, full_step, 0)
    acc = acc_ref[...]
    o_ref[...] = _bf(jnp.concatenate([acc[:tq], acc[tq:]], axis=1))


def sb_prompt_call(tok, sbq, sbk_b, sbv_b):
    tq = SBP_TQ
    nq = tok.seq // tq
    return pl.pallas_call(
        _sb_prompt_body,
        grid=(tok.batch, SB_KV, nq),
        in_specs=[pl.BlockSpec((tq, SB_GROUP * SB_DIM), lambda b, h, i: (b * nq + i, h)),
                  pl.BlockSpec((tok.seq, SB_DIM), lambda b, h, i: (b, h)),
                  pl.BlockSpec((tok.seq, SB_DIM), lambda b, h, i: (b, h))],
        out_specs=pl.BlockSpec((tq, SB_GROUP * SB_DIM), lambda b, h, i: (b * nq + i, h)),
        out_shape=SDS((tok.n_p, SB_HEADS * SB_DIM), BF16),
        scratch_shapes=[pltpu.VMEM((2 * tq, SB_DIM), F32), pltpu.VMEM((2 * tq, 1), F32)],
        compiler_params=_cp(("arbitrary", "arbitrary", "arbitrary"), 40),
        name="sb_prompt",
    )(sbq, sbk_b, sbv_b)


MLAP_TQ, MLAP_TK = 128, 256


def _mla_prompt_body(q_ref, km_ref, wv_ref, o_ref, m_ref, l_ref, acc_ref):
    tq, tk = MLAP_TQ, MLAP_TK
    nh = MLA_HEADS
    qi = pl.program_id(1)
    q = q_ref[...]
    qs = jnp.concatenate([q[:, h * MLA_QW:(h + 1) * MLA_QW] for h in range(nh)], axis=0)
    m_ref[...] = jnp.full_like(m_ref, NEG)
    l_ref[...] = jnp.zeros_like(l_ref)
    acc_ref[...] = jnp.zeros_like(acc_ref)
    qpos = qi * tq + (lax.broadcasted_iota(I32, (nh * tq, tk), 0) & (tq - 1))
    col = lax.broadcasted_iota(I32, (nh * tq, tk), 1)
    nkb = (qi * tq + tq - 1) // tk + 1
    nfull = (qi * tq + 1) // tk

    def block(kb, masked):
        ks = pl.multiple_of(kb * tk, tk)
        kblk = km_ref[pl.ds(ks, tk), :]
        s = _dot_nt(qs, kblk) * MLA_SCALE
        if masked:
            s = jnp.where((ks + col) <= qpos, s, NEG)
        m_old = m_ref[...]
        m_new = jnp.maximum(m_old, jnp.max(s, axis=1, keepdims=True))
        alpha = jnp.exp(m_old - m_new)
        p = jnp.exp(s - m_new)
        l_ref[...] = alpha * l_ref[...] + jnp.sum(p, axis=1, keepdims=True)
        acc_ref[...] = alpha * acc_ref[...] + _dot(_bf(p), kblk[:, :MLA_KVL])
        m_ref[...] = m_new

    def full_step(kb, carry):
        block(kb, False)
        return carry

    def masked_step(kb, carry):
        block(kb, True)
        return carry

    lax.fori_loop(0, nfull, full_step, 0)
    lax.fori_loop(nfull, nkb, masked_step, 0)
    o_lat = acc_ref[...] / l_ref[...]
    for h in range(nh):
        o_ref[:, h * MLA_V:(h + 1) * MLA_V] = _bf(_dot(_bf(o_lat[h * tq:(h + 1) * tq]), _bf(wv_ref[h])))


def mla_prompt_call(tok, qm, km, wv):
    tq = MLAP_TQ
    nq = tok.seq // tq
    return pl.pallas_call(
        _mla_prompt_body,
        grid=(tok.batch, nq),
        in_specs=[pl.BlockSpec((tq, MLA_HEADS * MLA_QW), lambda b, i: (b * nq + i, 0)),
                  pl.BlockSpec((tok.seq, MLA_QW), lambda b, i: (b, 0)),
                  pl.BlockSpec((MLA_HEADS, MLA_KVL, MLA_V), lambda b, i: (0, 0, 0))],
        out_specs=pl.BlockSpec((tq, MLA_HEADS * MLA_V), lambda b, i: (b * nq + i, 0)),
        out_shape=SDS((tok.n_p, MLA_HEADS * MLA_V), BF16),
        scratch_shapes=[pltpu.VMEM((MLA_HEADS * tq, 1), F32), pltpu.VMEM((MLA_HEADS * tq, 1), F32),
                        pltpu.VMEM((MLA_HEADS * tq, MLA_KVL), F32)],
        compiler_params=_cp(("arbitrary", "arbitrary"), 40),
        name="mla_prompt",
    )(qm, km, wv)


DEC_PP = 8
DEC_SB_KEYS = 256
DEC_QROWS = 8


def _decode_pages_per_step(n_pages):
    pp = DEC_PP
    while pp > 1 and n_pages % (2 * pp) != 0:
        pp //= 2
    assert n_pages % (2 * pp) == 0
    return pp


def _decode_body(pt_ref, qsb_ref, qml_ref, nk_ref, nv_ref, nl_ref, nr_ref, wv_ref,
                 ck_hbm, cv_hbm, cl_hbm, cr_hbm, osb_ref, oml_ref,
                 kbuf, vbuf, lbuf, rbuf, sem, *, n_pages, layer):
    pp = _decode_pages_per_step(n_pages)
    nk = pp * PAGE
    b = pl.program_id(0)
    nb = pl.num_programs(0)
    steps = n_pages // pp
    srows = SB_KV * DEC_QROWS
    mrows = MLA_HEADS * DEC_QROWS

    def copies(bb, s, slot):
        out = []
        for j in range(pp):
            page = pt_ref[bb * n_pages + s * pp + j]
            rows = pl.ds(j * PAGE, PAGE)
            for h in range(SB_KV):
                cols = pl.ds(h * SB_DIM, SB_DIM)
                out.append(pltpu.make_async_copy(ck_hbm.at[layer, page, :, h, :], kbuf.at[slot, rows, cols], sem.at[0, slot]))
                out.append(pltpu.make_async_copy(cv_hbm.at[layer, page, :, h, :], vbuf.at[slot, rows, cols], sem.at[1, slot]))
            out.append(pltpu.make_async_copy(cl_hbm.at[layer, page], lbuf.at[slot, rows], sem.at[2, slot]))
            out.append(pltpu.make_async_copy(cr_hbm.at[layer, page], rbuf.at[slot, :, rows], sem.at[3, slot]))
        return out

    @pl.when(b == 0)
    def _():
        for cp in copies(0, steps - 1, (steps - 1) % 2):
            cp.start()

    qsb = qsb_ref[...]
    qml = qml_ref[...]
    q_lat = qml[:, :MLA_KVL]
    q_rope = qml[:, MLA_KVL:MLA_KVL + MLA_ROPE]

    tq_sb = lax.broadcasted_iota(I32, (srows, PAGE), 0) & (DEC_QROWS // SB_GROUP - 1)
    tq_ml = lax.broadcasted_iota(I32, (mrows, PAGE), 0) & (DEC_QROWS - 1)
    col_sb = lax.broadcasted_iota(I32, (srows, PAGE), 1)
    col_ml = lax.broadcasted_iota(I32, (mrows, PAGE), 1)
    z = _dot_nt(qsb, _bf(nk_ref[...])) * SB_SCALE
    w, c0 = _sb_block(z, col_sb < tq_sb, _upper_mask(PAGE), jnp.zeros((srows, 1), F32))
    acc0 = _dot(_bf(w), _bf(nv_ref[...]))

    nl = _bf(nl_ref[...])
    s = (_dot_nt(q_lat, nl) + _dot(q_rope, _bf(nr_ref[...]))) * MLA_SCALE
    s = jnp.where(col_ml <= tq_ml, s, NEG)
    m0 = jnp.max(s, axis=1, keepdims=True)
    p = jnp.exp(s - m0)
    l0 = jnp.sum(p, axis=1, keepdims=True)
    macc0 = _dot(_bf(p), nl)

    sbk = min(nk, DEC_SB_KEYS)
    nsb = nk // sbk
    u_bf = _upper_mask(sbk)

    def step(it, carry):
        c, acc, m, l, macc = carry
        s_idx = steps - 1 - it
        slot = s_idx % 2
        for cp in copies(b, s_idx, slot):
            cp.wait()
        more = s_idx > 0
        for cp in copies(jnp.where(more, b, jnp.minimum(b + 1, nb - 1)), jnp.where(more, s_idx - 1, steps - 1), 1 - slot):
            cp.start()

        z = _dot_nt(qsb, _bf(kbuf[slot])) * SB_SCALE
        sp = _softplus(z)
        stack = lambda a: jnp.concatenate([a[:, i * sbk:(i + 1) * sbk] for i in range(nsb)], axis=0)
        lk = stack(-sp)
        hi, lo = _split_bf16(lk)
        after = _dot(hi, u_bf) + _dot(lo, u_bf)
        tot = jnp.sum(lk, axis=1, keepdims=True)
        carries = [None] * nsb
        for i in reversed(range(nsb)):
            carries[i] = c
            c = c + tot[i * srows:(i + 1) * srows]
        ws = jnp.exp(stack(z - sp) + after + jnp.concatenate(carries, axis=0))
        w = jnp.concatenate([ws[i * srows:(i + 1) * srows] for i in range(nsb)], axis=1)
        acc = acc + _dot(_bf(w), _bf(vbuf[slot]))

        lat = _bf(lbuf[slot])
        s = (_dot_nt(q_lat, lat) + _dot(q_rope, _bf(rbuf[slot]))) * MLA_SCALE
        m_new = jnp.maximum(m, jnp.max(s, axis=1, keepdims=True))
        alpha = jnp.exp(m - m_new)
        p = jnp.exp(s - m_new)
        l = alpha * l + jnp.sum(p, axis=1, keepdims=True)
        macc = alpha * macc + _dot(_bf(p), lat)
        return c, acc, m_new, l, macc

    c, acc, m, l, macc = lax.fori_loop(0, steps, step, (c0, acc0, m0, l0, macc0))

    @pl.when(b == nb - 1)
    def _():
        for cp in copies(b, steps - 1, (steps - 1) % 2):
            cp.wait()

    for h in range(SB_KV):
        osb_ref[h] = _bf(acc[h * DEC_QROWS:(h + 1) * DEC_QROWS, h * SB_DIM:(h + 1) * SB_DIM])
    o_lat = _bf(macc / l)
    for h in range(MLA_HEADS):
        oml_ref[:, h * MLA_V:(h + 1) * MLA_V] = _bf(_dot(o_lat[h * DEC_QROWS:(h + 1) * DEC_QROWS], _bf(wv_ref[h])))


def decode_call(page_table, qsb_d, qml_d, nk, nv, nl, nr, wv, cache_k, cache_v, cache_l, cache_r, layer):
    db, n_pages = page_table.shape
    pp = _decode_pages_per_step(n_pages)
    blk = lambda shp: pl.BlockSpec((None,) + shp, lambda b, pt: (b,) + (0,) * len(shp))
    grid_spec = pltpu.PrefetchScalarGridSpec(
        num_scalar_prefetch=1,
        grid=(db,),
        in_specs=[blk((SB_KV * DEC_QROWS, SB_KV * SB_DIM)), blk((MLA_HEADS * DEC_QROWS, MLA_QW)),
                  blk((PAGE, SB_KV * SB_DIM)), blk((PAGE, SB_KV * SB_DIM)), blk((PAGE, MLA_KVL)), blk((MLA_ROPE, PAGE)),
                  pl.BlockSpec((MLA_HEADS, MLA_KVL, MLA_V), lambda b, pt: (0, 0, 0)),
                  pl.BlockSpec(memory_space=pl.ANY), pl.BlockSpec(memory_space=pl.ANY),
                  pl.BlockSpec(memory_space=pl.ANY), pl.BlockSpec(memory_space=pl.ANY)],
        out_specs=[blk((SB_KV, DEC_QROWS, SB_DIM)), blk((DEC_QROWS, MLA_HEADS * MLA_V))],
        scratch_shapes=[pltpu.VMEM((2, pp * PAGE, SB_KV * SB_DIM), F32), pltpu.VMEM((2, pp * PAGE, SB_KV * SB_DIM), F32),
                        pltpu.VMEM((2, pp * PAGE, MLA_KVL), F32), pltpu.VMEM((2, MLA_ROPE, pp * PAGE), F32),
                        pltpu.SemaphoreType.DMA((4, 2))],
    )
    return pl.pallas_call(
        functools.partial(_decode_body, n_pages=n_pages, layer=layer),
        grid_spec=grid_spec,
        out_shape=[SDS((db, SB_KV, DEC_QROWS, SB_DIM), BF16), SDS((db, DEC_QROWS, MLA_HEADS * MLA_V), BF16)],
        compiler_params=_cp(("arbitrary",), 40),
        name="decode_attn",
    )(page_table.reshape(-1), qsb_d, qml_d, nk, nv, nl, nr, wv, cache_k, cache_v, cache_l, cache_r)


def _scan_chunk(q, k, v, g, s, valid_rows=None):
    c, dk = q.shape
    sub = min(SCAN_SUB, c)
    nsub = c // sub
    row = lax.broadcasted_iota(I32, (c, c), 0)
    col = lax.broadcasted_iota(I32, (c, c), 1)
    if valid_rows is not None:
        live = lax.broadcasted_iota(I32, (c, dk), 0) < valid_rows
        g = jnp.where(live, g, 0.0)
        k = jnp.where(live, k, 0.0)
    tri = jnp.where(row >= col, 1.0, 0.0).astype(BF16)
    g_hi, g_lo = _split_bf16(g)
    bcum = _dot(tri, g_hi) + _dot(tri, g_lo)
    b_last = bcum[c - 1:c, :]
    rowk = lax.broadcasted_iota(I32, (c, dk), 0)
    lhs, rhs = [], []
    for j in range(nsub):
        b0 = bcum[j * sub:j * sub + 1, :]
        lhs.append(_bf(q * jnp.exp(jnp.minimum(bcum - b0, 0.0))))
        in_j = (rowk >= j * sub) & (rowk < (j + 1) * sub)
        rhs.append(_bf(jnp.where(in_j, k * jnp.exp(jnp.where(in_j, b0 - bcum, 0.0)), 0.0)))
    att = _dot_nt(jnp.concatenate(lhs, axis=1), jnp.concatenate(rhs, axis=1))
    att = jnp.where(row >= col, att, 0.0)
    v_bf = _bf(v)
    o = _dot(_bf(att), v_bf) + _dot(_bf(q * jnp.exp(bcum)), _bf(s))
    kd = _bf(k * jnp.exp(b_last - bcum))
    decay_col = jnp.transpose(jnp.broadcast_to(jnp.exp(b_last), (8, dk)))[:, 0:1]
    s_new = decay_col * s + _dot_tn(kd, v_bf)
    return o, s_new


def _gla_inputs(q_ref, k_ref, ga_ref, wg_ref, bg_ref):
    q = q_ref[...] * (GLA_DK ** -0.5)
    k = k_ref[...]
    x = _dot(_bf(ga_ref[...]), _bf(wg_ref[...])) + bg_ref[...]
    g = -_softplus(-x) / GLA_NORM
    return q, k, g


def _hgrn_inputs(q_ref, f_ref, lb_ref):
    q = _silu(q_ref[...]) * (HGRN_DK ** -0.5)
    lb = lb_ref[...]
    f = lb + (1.0 - lb) * _sigmoid(f_ref[...])
    return q, 1.0 - f, jnp.log(f)


def _scan_out(o, gain_ref, r_ref):
    return _bf(_rms(o, gain_ref[...]) * _silu(r_ref[...]))


SCAN_TB = 512


def _scan_prompt_body(*refs, kind):
    if kind == "gla":
        q_ref, k_ref, v_ref, r_ref, ga_ref, wg_ref, bg_ref, gain_ref, o_ref, st_ref, s_ref, obuf = refs
        q, k, g = _gla_inputs(q_ref, k_ref, ga_ref, wg_ref, bg_ref)
    else:
        q_ref, f_ref, v_ref, r_ref, lb_ref, gain_ref, o_ref, st_ref, s_ref, obuf = refs
        q, k, g = _hgrn_inputs(q_ref, f_ref, lb_ref)
    v = v_ref[...]
    tb = pl.program_id(2)

    @pl.when(tb == 0)
    def _():
        s_ref[...] = jnp.zeros_like(s_ref)

    cc = SCAN_CHUNK
    for ci in range(SCAN_TB // cc):
        sl = slice(ci * cc, (ci + 1) * cc)
        o, s_new = _scan_chunk(q[sl], k[sl], v[sl], g[sl], s_ref[...])
        s_ref[...] = s_new
        obuf[sl, :] = o
    o_ref[...] = _scan_out(obuf[...], gain_ref, r_ref)

    @pl.when(tb == pl.num_programs(2) - 1)
    def _():
        st_ref[...] = s_ref[...]


def scan_prompt_call(tok, proj, ga_proj, w_gate_pad, b_gate, gain, lb, kind):
    tb = SCAN_TB
    ntb = tok.seq // tb
    rows = lambda w, c0: pl.BlockSpec((tb, w), lambda b, h, t: (b * ntb + t, c0 // w + h))
    if kind == "gla":
        nh, dk, dv = GLA_HEADS, GLA_DK, GLA_DV
        in_specs = [rows(dk, 0), rows(dk, 512), rows(dv, 1024), rows(dv, 2048),
                    pl.BlockSpec((tb, 128), lambda b, h, t: (b * ntb + t, 0)),
                    pl.BlockSpec((128, dk), lambda b, h, t: (0, h)), pl.BlockSpec((1, dk), lambda b, h, t: (0, h)),
                    pl.BlockSpec((1, dv), lambda b, h, t: (0, 0))]
        args = (proj, proj, proj, proj, ga_proj, w_gate_pad, b_gate.reshape(1, -1), gain.reshape(1, -1))
    else:
        nh, dk, dv = HGRN_HEADS, HGRN_DK, HGRN_DV
        in_specs = [rows(dk, 3072), rows(dk, 4096), rows(dv, 5120), rows(dv, 6144),
                    pl.BlockSpec((1, dk), lambda b, h, t: (0, h)), pl.BlockSpec((1, dv), lambda b, h, t: (0, 0))]
        args = (proj, proj, proj, proj, lb.reshape(1, -1), gain.reshape(1, -1))
    return pl.pallas_call(
        functools.partial(_scan_prompt_body, kind=kind),
        grid=(tok.batch, nh, ntb),
        in_specs=in_specs,
        out_specs=[pl.BlockSpec((tb, dv), lambda b, h, t: (b * ntb + t, h)),
                   pl.BlockSpec((None, None, dk, dv), lambda b, h, t: (b, h, 0, 0))],
        out_shape=[SDS((tok.n_p, nh * dv), BF16), SDS((tok.batch, nh, dk, dv), F32)],
        scratch_shapes=[pltpu.VMEM((dk, dv), F32), pltpu.VMEM((tb, dv), F32)],
        compiler_params=_cp(("arbitrary", "arbitrary", "arbitrary"), 40),
        name="scan_prompt_" + kind,
    )(*args)


SCAN_SROWS = 8


def _scan_sample_body(*refs, kind, valid):
    if kind == "gla":
        p_ref, ga_ref, wg_ref, bg_ref, gain_ref, s0_ref, o_ref, st_ref = refs
        nh, dk, dv = GLA_HEADS, GLA_DK, GLA_DV
    else:
        p_ref, lb_ref, gain_ref, s0_ref, o_ref, st_ref = refs
        nh, dk, dv = HGRN_HEADS, HGRN_DK, HGRN_DV
    for h in range(nh):
        if kind == "gla":
            q = p_ref[:, h * dk:(h + 1) * dk] * (GLA_DK ** -0.5)
            k = p_ref[:, 512 + h * dk:512 + (h + 1) * dk]
            v = p_ref[:, 1024 + h * dv:1024 + (h + 1) * dv]
            r = p_ref[:, 2048 + h * dv:2048 + (h + 1) * dv]
            x = _dot(_bf(ga_ref[...]), _bf(wg_ref[:, h * dk:(h + 1) * dk])) + bg_ref[:, h * dk:(h + 1) * dk]
            g = -_softplus(-x) / GLA_NORM
        else:
            q = _silu(p_ref[:, 3072 + h * dk:3072 + (h + 1) * dk]) * (HGRN_DK ** -0.5)
            lb = lb_ref[:, h * dk:(h + 1) * dk]
            f = lb + (1.0 - lb) * _sigmoid(p_ref[:, 4096 + h * dk:4096 + (h + 1) * dk])
            k = 1.0 - f
            g = jnp.log(f)
            v = p_ref[:, 5120 + h * dv:5120 + (h + 1) * dv]
            r = p_ref[:, 6144 + h * dv:6144 + (h + 1) * dv]
        o, s_new = _scan_chunk(q, k, v, g, s0_ref[h], valid_rows=valid)
        st_ref[h] = s_new
        o_ref[:, h * dv:(h + 1) * dv] = _bf(_rms(o, gain_ref[...]) * _silu(r))


def scan_sample_call(proj_s, ga_s, w_gate_pad, b_gate, gain, lb, s0, kind, valid):
    db = proj_s.shape[0]
    r = SCAN_SROWS
    full = lambda shp: pl.BlockSpec(shp, lambda b: (0,) * len(shp))
    per_b = lambda shp: pl.BlockSpec((None,) + shp, lambda b: (b,) + (0,) * len(shp))
    if kind == "gla":
        nh, dk, dv = GLA_HEADS, GLA_DK, GLA_DV
        in_specs = [per_b((r, proj_s.shape[2])), per_b((r, 128)), full((128, nh * dk)), full((1, nh * dk)), full((1, dv)),
                    per_b((nh, dk, dv))]
        args = (proj_s, ga_s, w_gate_pad, b_gate.reshape(1, -1), gain.reshape(1, -1), s0)
    else:
        nh, dk, dv = HGRN_HEADS, HGRN_DK, HGRN_DV
        in_specs = [per_b((r, proj_s.shape[2])), full((1, nh * dk)), full((1, dv)), per_b((nh, dk, dv))]
        args = (proj_s, lb.reshape(1, -1), gain.reshape(1, -1), s0)
    return pl.pallas_call(
        functools.partial(_scan_sample_body, kind=kind, valid=valid),
        grid=(db,),
        in_specs=in_specs,
        out_specs=[per_b((r, nh * dv)), per_b((nh, dk, dv))],
        out_shape=[SDS((db, r, nh * dv), BF16), SDS((db, nh, dk, dv), F32)],
        compiler_params=_cp(("arbitrary",), 40),
        name="scan_sample_" + kind,
    )(*args)


def _moe_prep_body(x_ref, g_ref, scp_ref, scs_ref, shp_ref, shs_ref, wr_ref, br_ref,
                   hp_ref, sel_ref, idx_ref, wt_ref, *, p_tiles):
    is_s = pl.program_id(0) >= p_tiles
    y = _rms(x_ref[...], g_ref[...])
    sc = jnp.where(is_s, scs_ref[...], scp_ref[...])
    sh = jnp.where(is_s, shs_ref[...], shp_ref[...])
    h = y * (1.0 + sc) + sh
    h_hi, h_lo = _split_bf16(h)
    half = D_MODEL // 2
    hr = h_hi.astype(F32)
    lo_bits = lax.shift_right_logical(lax.bitcast_convert_type(hr[:, :half], U32), jnp.uint32(16))
    hi_bits = lax.bitcast_convert_type(hr[:, half:], U32)
    hp_ref[...] = hi_bits | lo_bits
    w_hi, w_lo = _split_bf16(wr_ref[...])
    logits = _dot(h_hi, w_hi) + _dot(h_lo, w_hi) + _dot(h_hi, w_lo) + br_ref[...]
    t, ne = logits.shape
    lane = lax.broadcasted_iota(I32, (t, ne), 1)
    lane4 = lax.broadcasted_iota(I32, (t, TOP_K), 1)
    vals = logits
    sel = jnp.zeros((t, ne), F32)
    idx4 = jnp.zeros((t, TOP_K), I32)
    e4 = jnp.zeros((t, TOP_K), F32)
    m0 = None
    for kk in range(TOP_K):
        m = jnp.max(vals, axis=1, keepdims=True)
        ik = jnp.min(jnp.where(vals == m, lane, ne), axis=1, keepdims=True)
        hot = lane == ik
        vals = jnp.where(hot, -jnp.inf, vals)
        sel = jnp.where(hot, 1.0, sel)
        if kk == 0:
            m0 = m
        idx4 = jnp.where(lane4 == kk, ik, idx4)
        e4 = jnp.where(lane4 == kk, jnp.exp(m - m0), e4)
    sel_ref[...] = sel.astype(sel_ref.dtype)
    idx_ref[...] = idx4
    wt_ref[...] = e4 / jnp.sum(e4, axis=1, keepdims=True)


def moe_prep_call(tok, x, g, mod_p, mod_s, w_router, b_router):
    d = D_MODEL
    row = lambda w: pl.BlockSpec((TM, w), lambda i: (i, 0))
    return pl.pallas_call(
        functools.partial(_moe_prep_body, p_tiles=tok.p_tiles),
        grid=(tok.tiles,),
        in_specs=[row(d), pl.BlockSpec((1, d), lambda i: (0, 0))] + tok.mod_specs(4, d) + tok.mod_specs(3, d)
        + [pl.BlockSpec((d, N_EXPERTS), lambda i: (0, 0)), pl.BlockSpec((1, N_EXPERTS), lambda i: (0, 0))],
        out_specs=[row(d // 2), row(N_EXPERTS), row(TOP_K), row(TOP_K)],
        out_shape=[SDS((tok.n, d // 2), U32), SDS((tok.n, N_EXPERTS), BF16), SDS((tok.n, TOP_K), I32),
                   SDS((tok.n, TOP_K), F32)],
        compiler_params=_cp(("arbitrary",), 48),
        name="moe_prep",
    )(x, g.reshape(1, d), mod_p, mod_s, mod_p, mod_s, w_router, b_router.reshape(1, -1))


RANK_TM = 256


def _moe_rank_body(sel_ref, rank_ref, cnt_ref):
    @pl.when(pl.program_id(0) == 0)
    def _():
        cnt_ref[...] = jnp.zeros_like(cnt_ref)

    sel = sel_ref[...]
    t = sel.shape[0]
    r = lax.broadcasted_iota(I32, (t, t), 0)
    c = lax.broadcasted_iota(I32, (t, t), 1)
    before = jnp.where(r > c, 1.0, 0.0).astype(BF16)
    base = cnt_ref[...]
    rank_ref[...] = base + _dot(before, sel)
    cnt_ref[...] = base + jnp.sum(sel.astype(F32), axis=0, keepdims=True)


def moe_rank_call(sel):
    n, ne = sel.shape
    t = RANK_TM
    return pl.pallas_call(
        _moe_rank_body,
        grid=(n // t,),
        in_specs=[pl.BlockSpec((t, ne), lambda i: (i, 0))],
        out_specs=[pl.BlockSpec((t, ne), lambda i: (i, 0)), pl.BlockSpec((1, ne), lambda i: (0, 0))],
        out_shape=[SDS((n, ne), F32), SDS((1, ne), F32)],
        compiler_params=_cp(("arbitrary",), 32),
        name="moe_rank",
    )(sel)


DISP_TM = 256


def _zero_fill(used, tail_ref, dst_hbm, zrows, sem):
    n_rows = dst_hbm.shape[0]
    zrows[...] = jnp.zeros_like(zrows)

    def zcopy(row):
        return pltpu.make_async_copy(zrows, dst_hbm.at[pl.ds(pl.multiple_of(row, ROW_ALIGN), ROW_ALIGN)], sem)

    for act in ("start", "wait"):
        if tail_ref is not None:
            for e in range(N_EXPERTS):
                @pl.when(tail_ref[e] >= 0)
                def _(e=e, act=act):
                    getattr(zcopy(tail_ref[e]), act)()

        for r in range(N_EXPERTS):
            @pl.when(used + r * ROW_ALIGN < n_rows)
            def _(r=r, act=act):
                getattr(zcopy(used + r * ROW_ALIGN), act)()


def _moe_dispatch_body(tail_ref, pos_ref, h_ref, xs_hbm, zrows, sem):
    i = pl.program_id(0)
    t = DISP_TM

    @pl.when(i == 0)
    def _():
        _zero_fill(tail_ref[N_EXPERTS], tail_ref, xs_hbm, zrows, sem)

    def copy(tt, kk):
        return pltpu.make_async_copy(h_ref.at[pl.ds(tt, 1)], xs_hbm.at[pl.ds(pos_ref[tt * TOP_K + kk], 1)], sem)

    def issue(tt, c):
        for kk in range(TOP_K):
            copy(tt, kk).start()
        return c

    def drain(tt, c):
        for kk in range(TOP_K):
            copy(tt, kk).wait()
        return c

    lax.fori_loop(0, t, issue, 0)
    lax.fori_loop(0, t, drain, 0)


def moe_dispatch_call(tail_rows, pos_flat, hp, n_rows):
    n, w = hp.shape
    t = DISP_TM
    grid_spec = pltpu.PrefetchScalarGridSpec(
        num_scalar_prefetch=1,
        grid=(n // t,),
        in_specs=[pl.BlockSpec((t * TOP_K,), lambda i, tl: (i,), memory_space=pltpu.SMEM),
                  pl.BlockSpec((t, w), lambda i, tl: (i, 0))],
        out_specs=pl.BlockSpec(memory_space=pl.ANY),
        scratch_shapes=[pltpu.VMEM((ROW_ALIGN, w), U32), pltpu.SemaphoreType.DMA(())],
    )
    return pl.pallas_call(
        _moe_dispatch_body,
        grid_spec=grid_spec,
        out_shape=SDS((n_rows, w), U32),
        compiler_params=_cp(("arbitrary",), 32),
        name="moe_dispatch",
    )(tail_rows, pos_flat, hp)


def _row_copies(src, src_off, dst, dst_off, nrows, sem, chunk, act):
    nfull = nrows // chunk

    def body(j, c):
        o = j * chunk
        act(pltpu.make_async_copy(src.at[pl.ds(pl.multiple_of(src_off + o, ROW_ALIGN), chunk)],
                                  dst.at[pl.ds(pl.multiple_of(dst_off + o, ROW_ALIGN), chunk)], sem))
        return c

    lax.fori_loop(0, nfull, body, 0)
    rem = nrows - nfull * chunk
    done = nfull * chunk
    bit = chunk // 2
    while bit >= ROW_ALIGN:
        take = (rem & bit) != 0

        @pl.when(take)
        def _(done=done, bit=bit):
            act(pltpu.make_async_copy(src.at[pl.ds(pl.multiple_of(src_off + done, ROW_ALIGN), bit)],
                                      dst.at[pl.ds(pl.multiple_of(dst_off + done, ROW_ALIGN), bit)], sem))

        done = done + jnp.where(take, bit, 0)
        bit //= 2


def _moe_expert_body(ie_ref, if_ref, start_ref, rows_ref, used_ref,
                     xs_hbm, wg_ref, wl_ref, bg_ref, bl_ref, wd_ref, bd_ref, ys_hbm,
                     xin, yacc, wgb, wlb, wdb, zrows, sem):
    it = pl.program_id(0)
    f = pl.program_id(1)
    nf = pl.num_programs(1)
    rows = rows_ref[it]
    start = start_ref[it]
    half = D_MODEL // 2
    sub = MOE_SUB
    hsub = sub // 2
    n_full = rows // sub
    rem = rows - n_full * sub
    n_big = n_full + jnp.where(rem > hsub, 1, 0)
    has_small = jnp.logical_and(rem > 0, rem <= hsub)

    @pl.when(jnp.logical_and(it == 0, f == 0))
    def _():
        xin[...] = jnp.zeros_like(xin)

    @pl.when(rows > 0)
    def _():
        @pl.when(f == 0)
        def _():
            _row_copies(xs_hbm, start, xin, 0, rows, sem.at[0], sub, lambda cp: cp.start())

            def init(r, c):
                yacc[pl.ds(pl.multiple_of(r * hsub, hsub), hsub), :] = jnp.broadcast_to(bd_ref[...], (hsub, D_MODEL))
                return c

            lax.fori_loop(0, (rows + hsub - 1) // hsub, init, 0)
            _row_copies(xs_hbm, start, xin, 0, rows, sem.at[0], sub, lambda cp: cp.wait())

        wgb[...] = _bf(wg_ref[...])
        wlb[...] = _bf(wl_ref[...])
        wdb[...] = _bf(wd_ref[...])

        def tile(o, n):
            w = xin[pl.ds(o, n), :]
            x_lo = _bf(lax.bitcast_convert_type(lax.shift_left(w, jnp.uint32(16)), F32))
            x_hi = _bf(lax.bitcast_convert_type(w & jnp.uint32(0xFFFF0000), F32))
            gate = _dot(x_lo, wgb[0:half, :]) + _dot(x_hi, wgb[half:, :]) + bg_ref[...]
            lin = _dot(x_lo, wlb[0:half, :]) + _dot(x_hi, wlb[half:, :]) + bl_ref[...]
            gate = jnp.minimum(gate, SWIGLU_LIMIT)
            lin = jnp.clip(lin, -SWIGLU_LIMIT, SWIGLU_LIMIT)
            act = (lin + 1.0) * gate * _sigmoid(SWIGLU_ALPHA * gate)
            yacc[pl.ds(o, n), :] += _dot(_bf(act), wdb[...])

        def pair(p, c):
            o = pl.multiple_of(p * (2 * sub), 2 * sub)
            tile(o, sub)
            tile(o + sub, sub)
            return c

        lax.fori_loop(0, n_big // 2, pair, 0)

        @pl.when(n_big % 2 == 1)
        def _():
            tile(pl.multiple_of((n_big - 1) * sub, sub), sub)

        @pl.when(has_small)
        def _():
            tile(pl.multiple_of(n_full * sub, sub), hsub)

        @pl.when(f == nf - 1)
        def _():
            _row_copies(yacc, 0, ys_hbm, start, rows, sem.at[1], sub, lambda cp: cp.start())
            _row_copies(yacc, 0, ys_hbm, start, rows, sem.at[1], sub, lambda cp: cp.wait())

    @pl.when(jnp.logical_and(it == pl.num_programs(0) - 1, f == nf - 1))
    def _():
        _zero_fill(used_ref[0], None, ys_hbm, zrows, sem.at[1])


def moe_expert_call(layer, item_e, item_f, item_start, item_rows, used, xs, w_gate_up, b_gate_up, w_down, b_down):
    nl, ne, d, ff2 = w_gate_up.shape
    ff = ff2 // 2
    nf = ff // MOE_F
    n_items = item_e.shape[0]
    n_rows = xs.shape[0]

    def wmap(col0, kind):
        def index_map(i, f, ie, ife, st, rw, us):
            fi = jnp.where(ife[i] >= 0, f, nf - 1)
            if kind == "col":
                return layer, ie[i], 0, col0 + fi
            if kind == "row":
                return layer, ie[i], fi, 0
            return layer, ie[i], 0, 0
        return index_map

    grid_spec = pltpu.PrefetchScalarGridSpec(
        num_scalar_prefetch=5,
        grid=(n_items, nf),
        in_specs=[pl.BlockSpec(memory_space=pl.ANY),
                  pl.BlockSpec((None, None, d, MOE_F), wmap(0, "col")),
                  pl.BlockSpec((None, None, d, MOE_F), wmap(nf, "col")),
                  pl.BlockSpec((None, None, 1, MOE_F), wmap(0, "col")),
                  pl.BlockSpec((None, None, 1, MOE_F), wmap(nf, "col")),
                  pl.BlockSpec((None, None, MOE_F, d), wmap(0, "row")),
                  pl.BlockSpec((None, None, 1, d), wmap(0, "fixed"))],
        out_specs=pl.BlockSpec(memory_space=pl.ANY),
        scratch_shapes=[pltpu.VMEM((MOE_ROWS, d // 2), U32), pltpu.VMEM((MOE_ROWS, d), F32),
                        pltpu.VMEM((d, MOE_F), BF16), pltpu.VMEM((d, MOE_F), BF16), pltpu.VMEM((MOE_F, d), BF16),
                        pltpu.VMEM((ROW_ALIGN, d), F32), pltpu.SemaphoreType.DMA((2,))],
    )
    return pl.pallas_call(
        _moe_expert_body,
        grid_spec=grid_spec,
        out_shape=SDS((n_rows, d), F32),
        compiler_params=_cp(("arbitrary", "arbitrary"), MOE_VMEM_MIB),
        name="moe_experts",
    )(item_e, item_f, item_start, item_rows, used, xs, w_gate_up, w_gate_up,
      b_gate_up.reshape(nl, ne, 1, ff2), b_gate_up.reshape(nl, ne, 1, ff2), w_down, b_down.reshape(nl, ne, 1, d))


COMB_TM = 256


def _moe_combine_body(pos_ref, ys_hbm, x_ref, wt_ref, gp_ref, gs_ref, gf_ref, o_ref, y_ref, buf, sem, *, p_tiles, tpb):
    i = pl.program_id(0)
    t = COMB_TM

    def copy(tt, kk):
        return pltpu.make_async_copy(ys_hbm.at[pl.ds(pos_ref[tt * TOP_K + kk], 1)], buf.at[kk, pl.ds(tt, 1)], sem)

    def issue(tt, c):
        for kk in range(TOP_K):
            copy(tt, kk).start()
        return c

    def drain(tt, c):
        for kk in range(TOP_K):
            copy(tt, kk).wait()
        return c

    lax.fori_loop(0, t, issue, 0)
    lax.fori_loop(0, t, drain, 0)
    wt = wt_ref[...]
    moe = wt[:, 0:1] * buf[0]
    for kk in range(1, TOP_K):
        moe = moe + wt[:, kk:kk + 1] * buf[kk]
    gate = jnp.where(i >= p_tiles * tpb, gs_ref[...], gp_ref[...])
    x = x_ref[...] + gate * moe
    o_ref[...] = x
    y_ref[...] = _rms(x, gf_ref[...])


def moe_combine_call(tok, pos_flat, ys, x, wt, mod_p, mod_s, g_final):
    d = D_MODEL
    t = COMB_TM
    tpb = TM // t
    row = lambda w: pl.BlockSpec((t, w), lambda i: (i, 0))

    def pmap(i):
        return jnp.minimum(i // (tok.tiles_per_batch * tpb), tok.batch - 1), 0, 5

    def smap(i):
        return jnp.maximum(i - tok.p_tiles * tpb, 0), 5

    return pl.pallas_call(
        functools.partial(_moe_combine_body, p_tiles=tok.p_tiles, tpb=tpb),
        grid=(tok.n // t,),
        in_specs=[pl.BlockSpec((t * TOP_K,), lambda i: (i,), memory_space=pltpu.SMEM),
                  pl.BlockSpec(memory_space=pl.ANY), row(d), row(TOP_K),
                  pl.BlockSpec((None, 1, d), pmap), pl.BlockSpec((t, d), smap),
                  pl.BlockSpec((1, d), lambda i: (0, 0))],
        out_specs=[row(d), row(d)],
        out_shape=[SDS((tok.n, d), F32), SDS((tok.n, d), F32)],
        scratch_shapes=[pltpu.VMEM((TOP_K, t, d), F32), pltpu.SemaphoreType.DMA(())],
        compiler_params=_cp(("arbitrary",), 48),
        name="moe_combine",
    )(pos_flat, ys, x, wt, mod_p, mod_s, g_final.reshape(1, d))


def moe_layer(tok, layer, x, g, mod_p, mod_s, w_router, b_router, w_gate_up, b_gate_up, w_down, b_down, g_final):
    hp, sel, idx4, wt = moe_prep_call(tok, x, g, mod_p, mod_s, w_router[layer], b_router[layer])
    rank, cnt = moe_rank_call(sel)
    real = cnt[0].astype(I32)
    counts = (real + ROW_ALIGN - 1) // ROW_ALIGN * ROW_ALIGN
    offs = jnp.cumsum(counts) - counts
    pos = jnp.take_along_axis(rank.astype(I32) + offs[None, :], idx4, axis=1)
    pos_flat = pos.reshape(-1)
    tail_rows = jnp.where(real > 0, offs + counts - ROW_ALIGN, -1).astype(I32)
    n_rows = tok.n * TOP_K + N_EXPERTS * ROW_ALIGN
    n_items = N_EXPERTS + n_rows // MOE_ROWS
    passes = (counts + MOE_ROWS - 1) // MOE_ROWS
    pend = jnp.cumsum(passes)
    total = pend[-1]
    ii = jnp.arange(n_items, dtype=I32)
    e_of = jnp.minimum(jnp.sum((pend[None, :] <= ii[:, None]).astype(I32), axis=1), N_EXPERTS - 1)
    j_of = ii - (pend - passes)[e_of]
    valid = ii < total
    last_e = e_of[jnp.maximum(total - 1, 0)]
    item_e = jnp.where(valid, e_of, last_e)
    item_f = jnp.where(valid, 0, -1).astype(I32)
    item_start = jnp.where(valid, offs[e_of] + j_of * MOE_ROWS, 0).astype(I32)
    item_rows = jnp.where(valid, jnp.minimum(counts[e_of] - j_of * MOE_ROWS, MOE_ROWS), 0).astype(I32)
    used = jnp.sum(counts).astype(I32).reshape(1)
    xs = moe_dispatch_call(jnp.concatenate([tail_rows, used]), pos_flat, hp, n_rows)
    ys = moe_expert_call(layer, item_e, item_f, item_start, item_rows, used, xs, w_gate_up, b_gate_up, w_down, b_down)
    return moe_combine_call(tok, pos_flat, ys, x, wt, mod_p, mod_s, g_final)


def _rope_tables(pos):
    half = MLA_ROPE // 2
    inv = ROPE_THETA ** (-jnp.arange(half, dtype=F32) / half)
    ang = pos.astype(F32)[:, None] * inv
    cos, sin = jnp.cos(ang), jnp.sin(ang)
    z = jnp.zeros((pos.shape[0], 128 - MLA_ROPE), F32)
    return jnp.concatenate([cos, cos, z], axis=1), jnp.concatenate([-sin, sin, z], axis=1)


def _swap_halves(w):
    half = w.shape[-1] // 2
    return jnp.concatenate([w[..., half:], w[..., :half]], axis=-1)


def kernel(x_prompt, x_sample, cache_sb_k, cache_sb_v, cache_mla_latent, cache_mla_krope, state_gla, state_hgrn, page_table, c_prompt, c_sample, w_ada, b_ada, g_norm, w_in_ab, g_q_a, w_q_b, g_kv_a, w_kv_b, w_out_ab, w_in_cd, w_gla_gate, b_gla_gate, g_gla_out, hgrn_lb, g_hgrn_out, w_out_cd, w_router, b_router, w_gate_up, b_gate_up, w_down, b_down, g_final):
    batch, seq, d = x_prompt.shape
    db, ds, _ = x_sample.shape
    depth = w_ada.shape[0]
    past_len = page_table.shape[1] * PAGE
    tok = Tok(batch, seq, db, ds)
    assert d == D_MODEL and ds <= DEC_QROWS // SB_GROUP

    x = jnp.concatenate([x_prompt.reshape(tok.n_p, d), x_sample.reshape(tok.n_s, d)], axis=0)
    n_c = batch + db
    c_all = jnp.concatenate([c_prompt, c_sample, jnp.zeros((-n_c % 8, d), F32)], axis=0)
    mod = adaln_call(c_all, w_ada, b_ada)

    pos_all = jnp.concatenate([jnp.tile(jnp.arange(seq, dtype=I32), batch),
                               jnp.tile(past_len + jnp.arange(ds, dtype=I32), db)])
    cos_t, sin_t = _rope_tables(pos_all)
    lb_soft = jax.nn.softmax(hgrn_lb.astype(F32), axis=0)
    lower_bounds = jnp.cumsum(lb_soft, axis=0) - lb_soft[0]

    outs_ab, outs_cd = [], []
    y_final = None
    for l in range(depth):
        j = l // 2
        mod_p = mod[l, :batch].reshape(batch, 1, 6 * d)
        mod_s = jnp.repeat(mod[l, batch:n_c], ds, axis=0)
        h = normmod_call(tok, x, g_norm[l, 0], mod_p, mod_s, 0, 1)
        if l % 2 == 0:
            w_in = w_in_ab[j]
            zc = jnp.zeros((d, 128 - MLA_ROPE), F32)
            w_kr = w_in[:, 2816:2880]
            w_aug = jnp.concatenate([w_in[:, :2816], w_kr, zc, _swap_halves(w_kr), zc], axis=1)
            proj = matmul_call(h, w_aug, 1024, "in_proj_ab")
            wq = w_q_b[j].reshape(MLA_QL, MLA_HEADS, MLA_NOPE + MLA_ROPE)
            zq = jnp.zeros((MLA_QL, MLA_HEADS, 128 - MLA_ROPE), F32)
            wq_rope = wq[:, :, MLA_NOPE:]
            wq_aug = jnp.concatenate([wq[:, :, :MLA_NOPE].reshape(MLA_QL, -1),
                                      jnp.concatenate([wq_rope, zq], axis=2).reshape(MLA_QL, -1),
                                      jnp.concatenate([_swap_halves(wq_rope), zq], axis=2).reshape(MLA_QL, -1)], axis=1)
            wk_abs = jnp.transpose(w_kv_b[j][:, :, :MLA_NOPE], (1, 2, 0))
            wv = jnp.transpose(w_kv_b[j][:, :, MLA_NOPE:], (1, 0, 2))
            sbq, sbk, sbv, sbk_b, sbv_b, lat, krope, qm, km = ab_post_call(proj, cos_t, sin_t, g_q_a[j], g_kv_a[j], wq_aug, wk_abs)
            o_sb_p = sb_prompt_call(tok, sbq, sbk_b, sbv_b)
            o_ml_p = mla_prompt_call(tok, qm, km, wv)
            qsb_d = sbq[tok.n_p:].reshape(db, ds, SB_KV, SB_GROUP, SB_DIM).transpose(0, 2, 3, 1, 4)
            qsb_d = jnp.pad(qsb_d, ((0, 0), (0, 0), (0, 0), (0, DEC_QROWS // SB_GROUP - ds), (0, 0)))
            qsb_d = qsb_d.reshape(db, SB_KV, DEC_QROWS, SB_DIM)
            head_eye = jnp.eye(SB_KV, dtype=BF16)[None, :, None, :, None]
            qsb_d = (qsb_d[:, :, :, None, :] * head_eye).reshape(db, SB_KV * DEC_QROWS, SB_KV * SB_DIM)
            qml_d = qm[tok.n_p:].reshape(db, ds, MLA_HEADS, MLA_QW).transpose(0, 2, 1, 3)
            qml_d = jnp.pad(qml_d, ((0, 0), (0, 0), (0, DEC_QROWS - ds), (0, 0))).reshape(db, MLA_HEADS * DEC_QROWS, MLA_QW)
            padk = lambda a: jnp.pad(a[tok.n_p:].reshape(db, ds, -1), ((0, 0), (0, PAGE - ds), (0, 0)))
            o_sb_s, o_ml_s = decode_call(page_table, qsb_d, qml_d, padk(sbk), padk(sbv), padk(lat),
                                         jnp.swapaxes(padk(krope), 1, 2), wv,
                                         cache_sb_k, cache_sb_v, cache_mla_latent, jnp.swapaxes(cache_mla_krope, 2, 3), j)
            hs = DEC_QROWS // SB_GROUP
            o_sb_s = o_sb_s.reshape(db, SB_KV, SB_GROUP, hs, SB_DIM)[:, :, :, :ds].transpose(0, 3, 1, 2, 4).reshape(tok.n_s, -1)
            o_ml_s = o_ml_s[:, :ds].reshape(tok.n_s, -1)
            o = jnp.concatenate([jnp.concatenate([o_sb_p, o_ml_p], axis=1), jnp.concatenate([o_sb_s, o_ml_s], axis=1)], axis=0)
            x = matmul_res_call(tok, o, w_out_ab[j], x, mod_p, mod_s, 2, "out_proj_ab")
            outs_ab.append((sbk, sbv, lat, krope))
        else:
            w_in = w_in_cd[j]
            w_main = jnp.concatenate([w_in[:, :2048], w_in[:, 2064:]], axis=1)
            w_ga = jnp.concatenate([w_in[:, 2048:2064], jnp.zeros((d, 128 - GLA_RANK), F32)], axis=1)
            proj = matmul_call(h, w_main, 1024, "in_proj_cd")
            ga = matmul_call(h, w_ga, 128, "in_proj_gate")
            w_gate_pad = jnp.concatenate([w_gla_gate[j], jnp.zeros((128 - GLA_RANK, GLA_HEADS * GLA_DK), F32)], axis=0)
            lb = lower_bounds[l]
            o_gla_p, st_gla_p = scan_prompt_call(tok, proj, ga, w_gate_pad, b_gla_gate[j], g_gla_out[j], lb, "gla")
            o_hgrn_p, st_hgrn_p = scan_prompt_call(tok, proj, ga, w_gate_pad, b_gla_gate[j], g_hgrn_out[j], lb, "hgrn")
            pad_t = lambda a: jnp.pad(a[tok.n_p:].reshape(db, ds, -1), ((0, 0), (0, SCAN_SROWS - ds), (0, 0)))
            proj_s, ga_s = pad_t(proj), pad_t(ga)
            o_gla_s, st_gla_s = scan_sample_call(proj_s, ga_s, w_gate_pad, b_gla_gate[j], g_gla_out[j], lb, state_gla[j], "gla", ds)
            o_hgrn_s, st_hgrn_s = scan_sample_call(proj_s, ga_s, w_gate_pad, b_gla_gate[j], g_hgrn_out[j], lb, state_hgrn[j], "hgrn", ds)
            o_s = jnp.concatenate([o_gla_s[:, :ds].reshape(tok.n_s, -1), o_hgrn_s[:, :ds].reshape(tok.n_s, -1)], axis=1)
            o = jnp.concatenate([jnp.concatenate([o_gla_p, o_hgrn_p], axis=1), o_s], axis=0)
            x = matmul_res_call(tok, o, w_out_cd[j], x, mod_p, mod_s, 2, "out_proj_cd")
            outs_cd.append((st_gla_p, st_hgrn_p, st_gla_s, st_hgrn_s))
        x, y_final = moe_layer(tok, l, x, g_norm[l, 1], mod_p, mod_s, w_router, b_router, w_gate_up, b_gate_up,
                               w_down, b_down, g_final)

    y_prompt = y_final[:tok.n_p].reshape(batch, seq, d)
    y_sample = y_final[tok.n_p:].reshape(db, ds, d)

    def grp(a, s, shape):
        return a[s].reshape(shape)

    sp, ss = slice(0, tok.n_p), slice(tok.n_p, tok.n)
    sb_k_p = jnp.stack([grp(r[0], sp, (batch, seq, SB_KV, SB_DIM)) for r in outs_ab])
    sb_v_p = jnp.stack([grp(r[1], sp, (batch, seq, SB_KV, SB_DIM)) for r in outs_ab])
    lat_p = jnp.stack([grp(r[2], sp, (batch, seq, MLA_KVL)) for r in outs_ab])
    krope_p = jnp.stack([grp(r[3], sp, (batch, seq, MLA_ROPE)) for r in outs_ab])
    sb_k_s = jnp.stack([grp(r[0], ss, (db, ds, SB_KV, SB_DIM)) for r in outs_ab])
    sb_v_s = jnp.stack([grp(r[1], ss, (db, ds, SB_KV, SB_DIM)) for r in outs_ab])
    lat_s = jnp.stack([grp(r[2], ss, (db, ds, MLA_KVL)) for r in outs_ab])
    krope_s = jnp.stack([grp(r[3], ss, (db, ds, MLA_ROPE)) for r in outs_ab])
    gla_p = jnp.stack([r[0] for r in outs_cd])
    hgrn_p = jnp.stack([r[1] for r in outs_cd])
    gla_s = jnp.stack([r[2] for r in outs_cd])
    hgrn_s = jnp.stack([r[3] for r in outs_cd])
    return (y_prompt, y_sample, sb_k_p, sb_v_p, lat_p, krope_p, gla_p, hgrn_p,
            sb_k_s, sb_v_s, lat_s, krope_s, gla_s, hgrn_s)
```

```python
import functools
import math

import numpy as np
import jax
import jax.numpy as jnp
from jax import lax
from jax.experimental import pallas as pl
from jax.experimental.pallas import tpu as pltpu

F32 = jnp.float32
BF16 = jnp.bfloat16
I32 = jnp.int32
U32 = jnp.uint32
SDS = jax.ShapeDtypeStruct

D_MODEL = 2048
EPS = 1e-6
PAGE = 128
SB_HEADS, SB_KV, SB_GROUP, SB_DIM = 8, 4, 2, 128
MLA_HEADS, MLA_QL, MLA_KVL, MLA_NOPE, MLA_ROPE, MLA_V = 8, 512, 256, 128, 64, 128
MLA_SCALE = (MLA_NOPE + MLA_ROPE) ** -0.5
SB_SCALE = SB_DIM ** -0.5
ROPE_THETA = 10000.0
MLA_QW = MLA_KVL + 128
GLA_HEADS, GLA_DK, GLA_DV, GLA_RANK, GLA_NORM = 4, 128, 256, 16, 16.0
HGRN_HEADS, HGRN_DK, HGRN_DV = 8, 128, 128
SCAN_CHUNK = 64
SCAN_SUB = 16
N_EXPERTS, TOP_K = 32, 4
SWIGLU_LIMIT, SWIGLU_ALPHA = 7.0, 1.702
NEG = -1e30

TM = 512
MOE_ROWS = 1280
MOE_SUB = 256
MOE_F = 512
MOE_VMEM_MIB = 58
ROW_ALIGN = 8
MIB = 1 << 20


def _cp(sem, vmem_mib):
    return pltpu.CompilerParams(dimension_semantics=sem, vmem_limit_bytes=vmem_mib * MIB)


def _bf(x):
    return x.astype(BF16)


def _dot(a, b):
    return jnp.dot(a, b, preferred_element_type=F32)


def _dot_nt(a, b):
    return lax.dot_general(a, b, (((1,), (1,)), ((), ())), preferred_element_type=F32)


def _dot_tn(a, b):
    return lax.dot_general(a, b, (((0,), (0,)), ((), ())), preferred_element_type=F32)


def _sigmoid(x):
    return 1.0 / (1.0 + jnp.exp(-x))


def _silu(x):
    return x * _sigmoid(x)


def _softplus(x):
    return jnp.maximum(x, 0.0) + jnp.log(1.0 + jnp.exp(-jnp.abs(x)))


def _rms(x, g):
    return x * lax.rsqrt(jnp.mean(x * x, axis=-1, keepdims=True) + EPS) * g


def _split_bf16(x):
    hi = _bf(x)
    lo = _bf(x - hi.astype(F32))
    return hi, lo


class Tok:
    def __init__(self, batch, seq, dec_batch, dec_seq):
        self.batch, self.seq, self.dec_batch, self.dec_seq = batch, seq, dec_batch, dec_seq
        self.n_p = batch * seq
        self.n_s = dec_batch * dec_seq
        self.n = self.n_p + self.n_s
        assert seq % TM == 0 and self.n_s % TM == 0
        self.tiles_per_batch = seq // TM
        self.p_tiles = self.n_p // TM
        self.tiles = self.n // TM

    def mod_specs(self, k, width, col_of=None, grid_axis=0, ngrid=1):
        nblk = D_MODEL // width

        def pick(args):
            i = args[grid_axis]
            c = k * nblk + (col_of(*args) if col_of is not None else 0)
            return i, c

        def pmap(*args):
            i, c = pick(args)
            return jnp.minimum(i // self.tiles_per_batch, self.batch - 1), 0, c

        def smap(*args):
            i, c = pick(args)
            return jnp.maximum(i - self.p_tiles, 0), c

        return [pl.BlockSpec((None, 1, width), pmap), pl.BlockSpec((TM, width), smap)]


def _adaln_body(c_ref, w_ref, b_ref, o_ref):
    s = _bf(_silu(c_ref[...]))
    o_ref[...] = _dot(s, _bf(w_ref[...])) + b_ref[...]


def adaln_call(c_all, w_ada, b_ada):
    r = c_all.shape[0]
    nl, d, w6 = w_ada.shape
    tn = 1024
    return pl.pallas_call(
        _adaln_body,
        grid=(nl, w6 // tn),
        in_specs=[pl.BlockSpec((r, d), lambda l, n: (0, 0)),
                  pl.BlockSpec((None, d, tn), lambda l, n: (l, 0, n)),
                  pl.BlockSpec((None, 1, tn), lambda l, n: (l, 0, n))],
        out_specs=pl.BlockSpec((None, r, tn), lambda l, n: (l, 0, n)),
        out_shape=SDS((nl, r, w6), F32),
        compiler_params=_cp(("arbitrary", "arbitrary"), 40),
        name="adaln",
    )(c_all, w_ada, b_ada.reshape(nl, 1, w6))


def _normmod_body(x_ref, g_ref, scp_ref, scs_ref, shp_ref, shs_ref, o_ref, *, p_tiles):
    is_s = pl.program_id(0) >= p_tiles
    y = _rms(x_ref[...], g_ref[...])
    sc = jnp.where(is_s, scs_ref[...], scp_ref[...])
    sh = jnp.where(is_s, shs_ref[...], shp_ref[...])
    o_ref[...] = (y * (1.0 + sc) + sh).astype(o_ref.dtype)


def normmod_call(tok, x, g, mod_p, mod_s, k_shift, k_scale):
    d = D_MODEL
    return pl.pallas_call(
        functools.partial(_normmod_body, p_tiles=tok.p_tiles),
        grid=(tok.tiles,),
        in_specs=[pl.BlockSpec((TM, d), lambda i: (i, 0)), pl.BlockSpec((1, d), lambda i: (0, 0))]
        + tok.mod_specs(k_scale, d) + tok.mod_specs(k_shift, d),
        out_specs=pl.BlockSpec((TM, d), lambda i: (i, 0)),
        out_shape=SDS((tok.n, d), BF16),
        compiler_params=_cp(("arbitrary",), 56),
        name="normmod",
    )(x, g.reshape(1, d), mod_p, mod_s, mod_p, mod_s)


def _mm_body(a_ref, w_ref, o_ref, wbf_ref):
    @pl.when(pl.program_id(1) == 0)
    def _():
        wbf_ref[...] = _bf(w_ref[...])

    o_ref[...] = _dot(a_ref[...], wbf_ref[...])


def _mm_res_body(a_ref, w_ref, r_ref, gp_ref, gs_ref, o_ref, wbf_ref, *, p_tiles):
    @pl.when(pl.program_id(1) == 0)
    def _():
        wbf_ref[...] = _bf(w_ref[...])

    gate = jnp.where(pl.program_id(1) >= p_tiles, gs_ref[...], gp_ref[...])
    o_ref[...] = r_ref[...] + gate * _dot(a_ref[...], wbf_ref[...])


def matmul_call(a, w, tn, name):
    n, k = a.shape
    nout = w.shape[1]
    assert n % TM == 0 and nout % tn == 0
    return pl.pallas_call(
        _mm_body,
        grid=(nout // tn, n // TM),
        in_specs=[pl.BlockSpec((TM, k), lambda j, i: (i, 0)), pl.BlockSpec((k, tn), lambda j, i: (0, j))],
        out_specs=pl.BlockSpec((TM, tn), lambda j, i: (i, j)),
        out_shape=SDS((n, nout), F32),
        scratch_shapes=[pltpu.VMEM((k, tn), BF16)],
        compiler_params=_cp(("arbitrary", "arbitrary"), 48),
        name=name,
    )(a, w)


def matmul_res_call(tok, a, w, res, mod_p, mod_s, k_gate, name):
    n, k = a.shape
    nout = w.shape[1]
    tn = 1024
    return pl.pallas_call(
        functools.partial(_mm_res_body, p_tiles=tok.p_tiles),
        grid=(nout // tn, n // TM),
        in_specs=[pl.BlockSpec((TM, k), lambda j, i: (i, 0)), pl.BlockSpec((k, tn), lambda j, i: (0, j)),
                  pl.BlockSpec((TM, tn), lambda j, i: (i, j))]
        + tok.mod_specs(k_gate, tn, col_of=lambda j, i: j, grid_axis=1),
        out_specs=pl.BlockSpec((TM, tn), lambda j, i: (i, j)),
        out_shape=SDS((n, nout), F32),
        scratch_shapes=[pltpu.VMEM((k, tn), BF16)],
        compiler_params=_cp(("arbitrary", "arbitrary"), 48),
        name=name,
    )(a, w, res, mod_p, mod_s)


AB_TM = 256


def _ab_post_body(p_ref, cos_ref, sin_ref, gq_ref, gkv_ref, wq_ref, wk_ref,
                  sbq_ref, sbk_ref, sbv_ref, sbkb_ref, sbvb_ref, lat_ref, kr_ref, qm_ref, km_ref,
                  wqb_ref, wkb_ref):
    @pl.when(pl.program_id(0) == 0)
    def _():
        wqb_ref[...] = _bf(wq_ref[...])
        wkb_ref[...] = _bf(wk_ref[...])

    cos = cos_ref[...]
    sin = sin_ref[...]
    sbq_ref[...] = _bf(p_ref[:, 0:1024])
    sbk = p_ref[:, 1024:1536]
    sbv = p_ref[:, 1536:2048]
    sbk_ref[...] = sbk
    sbv_ref[...] = sbv
    sbkb_ref[...] = _bf(sbk)
    sbvb_ref[...] = _bf(sbv)
    lat = _rms(p_ref[:, 2560:2816], gkv_ref[...])
    lat_ref[...] = lat
    kr = p_ref[:, 2816:2944] * cos + p_ref[:, 2944:3072] * sin
    kr_ref[...] = kr[:, :MLA_ROPE]
    km_ref[:, 0:MLA_KVL] = _bf(lat)
    km_ref[:, MLA_KVL:MLA_QW] = _bf(kr)
    qn = _bf(_rms(p_ref[:, 2048:2560], gq_ref[...]))
    q2 = _dot(qn, wqb_ref[...])
    for h in range(MLA_HEADS):
        qlat = _dot(_bf(q2[:, h * 128:(h + 1) * 128]), wkb_ref[h])
        qr = q2[:, 1024 + h * 128:1024 + (h + 1) * 128] * cos + q2[:, 2048 + h * 128:2048 + (h + 1) * 128] * sin
        qm_ref[:, h * MLA_QW:h * MLA_QW + MLA_KVL] = _bf(qlat)
        qm_ref[:, h * MLA_QW + MLA_KVL:(h + 1) * MLA_QW] = _bf(qr)


def ab_post_call(proj, cos_t, sin_t, g_q_a, g_kv_a, wq_aug, wk_abs):
    n = proj.shape[0]
    t = AB_TM
    row = lambda w: pl.BlockSpec((t, w), lambda i: (i, 0))
    full = lambda shp: pl.BlockSpec(shp, lambda i: (0,) * len(shp))
    outs = [((n, 1024), BF16), ((n, 512), F32), ((n, 512), F32), ((n, 512), BF16), ((n, 512), BF16),
            ((n, MLA_KVL), F32), ((n, MLA_ROPE), F32), ((n, MLA_HEADS * MLA_QW), BF16), ((n, MLA_QW), BF16)]
    return pl.pallas_call(
        _ab_post_body,
        grid=(n // t,),
        in_specs=[row(3072), row(128), row(128), full((1, MLA_QL)), full((1, MLA_KVL)),
                  full((MLA_QL, 3072)), full((MLA_HEADS, MLA_NOPE, MLA_KVL))],
        out_specs=[row(s[1]) for s, _ in outs],
        out_shape=[SDS(s, dt) for s, dt in outs],
        scratch_shapes=[pltpu.VMEM((MLA_QL, 3072), BF16), pltpu.VMEM((MLA_HEADS, MLA_NOPE, MLA_KVL), BF16)],
        compiler_params=_cp(("arbitrary",), 48),
        name="ab_post",
    )(proj, cos_t, sin_t, g_q_a.reshape(1, -1), g_kv_a.reshape(1, -1), wq_aug, wk_abs)


def _sb_block(z, strict, u_bf, c):
    sp = _softplus(z)
    lk = -sp if strict is None else jnp.where(strict, -sp, 0.0)
    hi, lo = _split_bf16(lk)
    after = _dot(hi, u_bf) + _dot(lo, u_bf)
    w = jnp.exp(z - sp + after + c)
    if strict is not None:
        w = jnp.where(strict, w, 0.0)
    return w, c + jnp.sum(lk, axis=1, keepdims=True)


def _upper_mask(n):
    r = lax.broadcasted_iota(I32, (n, n), 0)
    c = lax.broadcasted_iota(I32, (n, n), 1)
    return jnp.where(r > c, 1.0, 0.0).astype(BF16)


SBP_TQ, SBP_TK = 256, 256


def _sb_prompt_body(q_ref, k_ref, v_ref, o_ref, acc_ref, c_ref):
    tq, tk = SBP_TQ, SBP_TK
    qi = pl.program_id(2)
    q = q_ref[...]
    q2 = jnp.concatenate([q[:, :SB_DIM], q[:, SB_DIM:]], axis=0)
    acc_ref[...] = jnp.zeros_like(acc_ref)
    c_ref[...] = jnp.zeros_like(c_ref)
    qpos = qi * tq + (lax.broadcasted_iota(I32, (2 * tq, tk), 0) & (tq - 1))
    col = lax.broadcasted_iota(I32, (2 * tq, tk), 1)
    u_bf = _upper_mask(tk)
    nkb = (qi * tq + tq - 1) // tk + 1
    nfull = (qi * tq) // tk

    def block(kb, masked):
        ks = pl.multiple_of(kb * tk, tk)
        k = k_ref[pl.ds(ks, tk), :]
        v = v_ref[pl.ds(ks, tk), :]
        z = _dot_nt(q2, k) * SB_SCALE
        w, c_new = _sb_block(z, ((ks + col) < qpos) if masked else None, u_bf, c_ref[...])
        acc_ref[...] += _dot(_bf(w), v)
        c_ref[...] = c_new

    def masked_step(it, carry):
        block(nkb - 1 - it, True)
        return carry

    def full_step(it, carry):
        block(nfull - 1 - it, False)
        return carry

    lax.fori_loop(0, nkb - nfull, masked_step, 0)
    lax.fori_loop(0, nfull, full_step, 0)
    acc = acc_ref[...]
    o_ref[...] = _bf(jnp.concatenate([acc[:tq], acc[tq:]], axis=1))


def sb_prompt_call(tok, sbq, sbk_b, sbv_b):
    tq = SBP_TQ
    nq = tok.seq // tq
    return pl.pallas_call(
        _sb_prompt_body,
        grid=(tok.batch, SB_KV, nq),
        in_specs=[pl.BlockSpec((tq, SB_GROUP * SB_DIM), lambda b, h, i: (b * nq + i, h)),
                  pl.BlockSpec((tok.seq, SB_DIM), lambda b, h, i: (b, h)),
                  pl.BlockSpec((tok.seq, SB_DIM), lambda b, h, i: (b, h))],
        out_specs=pl.BlockSpec((tq, SB_GROUP * SB_DIM), lambda b, h, i: (b * nq + i, h)),
        out_shape=SDS((tok.n_p, SB_HEADS * SB_DIM), BF16),
        scratch_shapes=[pltpu.VMEM((2 * tq, SB_DIM), F32), pltpu.VMEM((2 * tq, 1), F32)],
        compiler_params=_cp(("arbitrary", "arbitrary", "arbitrary"), 40),
        name="sb_prompt",
    )(sbq, sbk_b, sbv_b)


MLAP_TQ, MLAP_TK = 128, 512


def _mla_prompt_body(q_ref, km_ref, wv_ref, o_ref, m_ref, l_ref, acc_ref):
    tq, tk = MLAP_TQ, MLAP_TK
    nh = MLA_HEADS
    qi = pl.program_id(1)
    q = q_ref[...]
    qs = jnp.concatenate([q[:, h * MLA_QW:(h + 1) * MLA_QW] for h in range(nh)], axis=0)
    m_ref[...] = jnp.full_like(m_ref, NEG)
    l_ref[...] = jnp.zeros_like(l_ref)
    acc_ref[...] = jnp.zeros_like(acc_ref)
    qpos = qi * tq + (lax.broadcasted_iota(I32, (nh * tq, tk), 0) & (tq - 1))
    col = lax.broadcasted_iota(I32, (nh * tq, tk), 1)
    nkb = (qi * tq + tq - 1) // tk + 1
    nfull = (qi * tq + 1) // tk

    def block(kb, masked):
        ks = pl.multiple_of(kb * tk, tk)
        kblk = km_ref[pl.ds(ks, tk), :]
        s = _dot_nt(qs, kblk) * MLA_SCALE
        if masked:
            s = jnp.where((ks + col) <= qpos, s, NEG)
        m_old = m_ref[...]
        m_new = jnp.maximum(m_old, jnp.max(s, axis=1, keepdims=True))
        alpha = jnp.exp(m_old - m_new)
        p = jnp.exp(s - m_new)
        l_ref[...] = alpha * l_ref[...] + jnp.sum(p, axis=1, keepdims=True)
        acc_ref[...] = alpha * acc_ref[...] + _dot(_bf(p), kblk[:, :MLA_KVL])
        m_ref[...] = m_new

    def full_step(kb, carry):
        block(kb, False)
        return carry

    def masked_step(kb, carry):
        block(kb, True)
        return carry

    lax.fori_loop(0, nfull, full_step, 0)
    lax.fori_loop(nfull, nkb, masked_step, 0)
    o_lat = acc_ref[...] / l_ref[...]
    for h in range(nh):
        o_ref[:, h * MLA_V:(h + 1) * MLA_V] = _bf(_dot(_bf(o_lat[h * tq:(h + 1) * tq]), _bf(wv_ref[h])))


def mla_prompt_call(tok, qm, km, wv):
    tq = MLAP_TQ
    nq = tok.seq // tq
    return pl.pallas_call(
        _mla_prompt_body,
        grid=(tok.batch, nq),
        in_specs=[pl.BlockSpec((tq, MLA_HEADS * MLA_QW), lambda b, i: (b * nq + i, 0)),
                  pl.BlockSpec((tok.seq, MLA_QW), lambda b, i: (b, 0)),
                  pl.BlockSpec((MLA_HEADS, MLA_KVL, MLA_V), lambda b, i: (0, 0, 0))],
        out_specs=pl.BlockSpec((tq, MLA_HEADS * MLA_V), lambda b, i: (b * nq + i, 0)),
        out_shape=SDS((tok.n_p, MLA_HEADS * MLA_V), BF16),
        scratch_shapes=[pltpu.VMEM((MLA_HEADS * tq, 1), F32), pltpu.VMEM((MLA_HEADS * tq, 1), F32),
                        pltpu.VMEM((MLA_HEADS * tq, MLA_KVL), F32)],
        compiler_params=_cp(("arbitrary", "arbitrary"), 40),
        name="mla_prompt",
    )(qm, km, wv)


DEC_PP = 8
DEC_SB_KEYS = 256
DEC_QROWS = 8


def _decode_pages_per_step(n_pages):
    pp = DEC_PP
    while pp > 1 and n_pages % (2 * pp) != 0:
        pp //= 2
    assert n_pages % (2 * pp) == 0
    return pp


def _decode_body(pt_ref, qsb_ref, qml_ref, nk_ref, nv_ref, nl_ref, nr_ref, wv_ref,
                 ck_hbm, cv_hbm, cl_hbm, cr_hbm, osb_ref, oml_ref,
                 kbuf, vbuf, lbuf, rbuf, sem, *, n_pages, layer):
    pp = _decode_pages_per_step(n_pages)
    nk = pp * PAGE
    b = pl.program_id(0)
    nb = pl.num_programs(0)
    steps = n_pages // pp
    srows = SB_KV * DEC_QROWS
    mrows = MLA_HEADS * DEC_QROWS

    def copies(bb, s, slot):
        out = []
        for j in range(pp):
            page = pt_ref[bb * n_pages + s * pp + j]
            rows = pl.ds(j * PAGE, PAGE)
            out.append(pltpu.make_async_copy(ck_hbm.at[layer, page], kbuf.at[slot, rows], sem.at[0, slot]))
            out.append(pltpu.make_async_copy(cv_hbm.at[layer, page], vbuf.at[slot, rows], sem.at[1, slot]))
            out.append(pltpu.make_async_copy(cl_hbm.at[layer, page], lbuf.at[slot, rows], sem.at[2, slot]))
            out.append(pltpu.make_async_copy(cr_hbm.at[layer, page], rbuf.at[slot, :, rows], sem.at[3, slot]))
        return out

    @pl.when(b == 0)
    def _():
        for cp in copies(0, steps - 1, (steps - 1) % 2):
            cp.start()

    qsb = qsb_ref[...]
    qml = qml_ref[...]
    q_lat = qml[:, :MLA_KVL]
    q_rope = qml[:, MLA_KVL:MLA_KVL + MLA_ROPE]
    sbk = DEC_SB_KEYS
    nsb = nk * SB_KV // sbk
    u_bf = _upper_mask(sbk)

    def own_head(nrows):
        r = lax.broadcasted_iota(I32, (nrows, sbk), 0)
        cidx = lax.broadcasted_iota(I32, (nrows, sbk), 1)
        return ((r & (srows - 1)) // DEC_QROWS) == (cidx & (SB_KV - 1))

    tq_sb = lax.broadcasted_iota(I32, (srows, sbk), 0) & (DEC_QROWS // SB_GROUP - 1)
    tok_sb = lax.broadcasted_iota(I32, (srows, sbk), 1) // SB_KV
    tq_ml = lax.broadcasted_iota(I32, (mrows, PAGE), 0) & (DEC_QROWS - 1)
    col_ml = lax.broadcasted_iota(I32, (mrows, PAGE), 1)
    z = _dot_nt(qsb, _bf(nk_ref[0:sbk, :])) * SB_SCALE
    w, c0 = _sb_block(z, jnp.logical_and(own_head(srows), tok_sb < tq_sb), u_bf, jnp.zeros((srows, 1), F32))
    acc0 = _dot(_bf(w), _bf(nv_ref[0:sbk, :]))

    nl = _bf(nl_ref[...])
    s = (_dot_nt(q_lat, nl) + _dot(q_rope, _bf(nr_ref[...]))) * MLA_SCALE
    s = jnp.where(col_ml <= tq_ml, s, NEG)
    m0 = jnp.max(s, axis=1, keepdims=True)
    p = jnp.exp(s - m0)
    l0 = jnp.sum(p, axis=1, keepdims=True)
    macc0 = _dot(_bf(p), nl)

    own = own_head(nsb * srows)

    def step(it, carry):
        c, acc, m, l, macc = carry
        s_idx = steps - 1 - it
        slot = s_idx % 2
        for cp in copies(b, s_idx, slot):
            cp.wait()
        more = s_idx > 0
        for cp in copies(jnp.where(more, b, jnp.minimum(b + 1, nb - 1)), jnp.where(more, s_idx - 1, steps - 1), 1 - slot):
            cp.start()

        z = _dot_nt(qsb, _bf(kbuf[slot].reshape(nk * SB_KV, SB_DIM))) * SB_SCALE
        sp = _softplus(z)
        stack = lambda a: jnp.concatenate([a[:, i * sbk:(i + 1) * sbk] for i in range(nsb)], axis=0)
        lk = jnp.where(own, stack(-sp), 0.0)
        hi, lo = _split_bf16(lk)
        after = _dot(hi, u_bf) + _dot(lo, u_bf)
        tot = jnp.sum(lk, axis=1, keepdims=True)
        carries = [None] * nsb
        for i in reversed(range(nsb)):
            carries[i] = c
            c = c + tot[i * srows:(i + 1) * srows]
        ws = jnp.where(own, jnp.exp(stack(z - sp) + after + jnp.concatenate(carries, axis=0)), 0.0)
        w = jnp.concatenate([ws[i * srows:(i + 1) * srows] for i in range(nsb)], axis=1)
        acc = acc + _dot(_bf(w), _bf(vbuf[slot].reshape(nk * SB_KV, SB_DIM)))

        lat = _bf(lbuf[slot])
        s = (_dot_nt(q_lat, lat) + _dot(q_rope, _bf(rbuf[slot]))) * MLA_SCALE
        m_new = jnp.maximum(m, jnp.max(s, axis=1, keepdims=True))
        alpha = jnp.exp(m - m_new)
        p = jnp.exp(s - m_new)
        l = alpha * l + jnp.sum(p, axis=1, keepdims=True)
        macc = alpha * macc + _dot(_bf(p), lat)
        return c, acc, m_new, l, macc

    c, acc, m, l, macc = lax.fori_loop(0, steps, step, (c0, acc0, m0, l0, macc0))

    @pl.when(b == nb - 1)
    def _():
        for cp in copies(b, steps - 1, (steps - 1) % 2):
            cp.wait()

    for h in range(SB_KV):
        osb_ref[h] = _bf(acc[h * DEC_QROWS:(h + 1) * DEC_QROWS])
    o_lat = _bf(macc / l)
    for h in range(MLA_HEADS):
        oml_ref[:, h * MLA_V:(h + 1) * MLA_V] = _bf(_dot(o_lat[h * DEC_QROWS:(h + 1) * DEC_QROWS], _bf(wv_ref[h])))


def decode_call(page_table, qsb_d, qml_d, nk, nv, nl, nr, wv, cache_k, cache_v, cache_l, cache_r, layer):
    db, n_pages = page_table.shape
    pp = _decode_pages_per_step(n_pages)
    blk = lambda shp: pl.BlockSpec((None,) + shp, lambda b, pt: (b,) + (0,) * len(shp))
    grid_spec = pltpu.PrefetchScalarGridSpec(
        num_scalar_prefetch=1,
        grid=(db,),
        in_specs=[blk((SB_KV * DEC_QROWS, SB_DIM)), blk((MLA_HEADS * DEC_QROWS, MLA_QW)),
                  blk((PAGE * SB_KV, SB_DIM)), blk((PAGE * SB_KV, SB_DIM)), blk((PAGE, MLA_KVL)), blk((MLA_ROPE, PAGE)),
                  pl.BlockSpec((MLA_HEADS, MLA_KVL, MLA_V), lambda b, pt: (0, 0, 0)),
                  pl.BlockSpec(memory_space=pl.ANY), pl.BlockSpec(memory_space=pl.ANY),
                  pl.BlockSpec(memory_space=pl.ANY), pl.BlockSpec(memory_space=pl.ANY)],
        out_specs=[blk((SB_KV, DEC_QROWS, SB_DIM)), blk((DEC_QROWS, MLA_HEADS * MLA_V))],
        scratch_shapes=[pltpu.VMEM((2, pp * PAGE, SB_KV, SB_DIM), F32), pltpu.VMEM((2, pp * PAGE, SB_KV, SB_DIM), F32),
                        pltpu.VMEM((2, pp * PAGE, MLA_KVL), F32), pltpu.VMEM((2, MLA_ROPE, pp * PAGE), F32),
                        pltpu.SemaphoreType.DMA((4, 2))],
    )
    return pl.pallas_call(
        functools.partial(_decode_body, n_pages=n_pages, layer=layer),
        grid_spec=grid_spec,
        out_shape=[SDS((db, SB_KV, DEC_QROWS, SB_DIM), BF16), SDS((db, DEC_QROWS, MLA_HEADS * MLA_V), BF16)],
        compiler_params=_cp(("arbitrary",), 40),
        name="decode_attn",
    )(page_table.reshape(-1), qsb_d, qml_d, nk, nv, nl, nr, wv, cache_k, cache_v, cache_l, cache_r)


def _scan_chunk(q, k, v, g, s, valid_rows=None):
    c, dk = q.shape
    sub = min(SCAN_SUB, c)
    nsub = c // sub
    row = lax.broadcasted_iota(I32, (c, c), 0)
    col = lax.broadcasted_iota(I32, (c, c), 1)
    if valid_rows is not None:
        live = lax.broadcasted_iota(I32, (c, dk), 0) < valid_rows
        g = jnp.where(live, g, 0.0)
        k = jnp.where(live, k, 0.0)
    tri = jnp.where(row >= col, 1.0, 0.0).astype(BF16)
    g_hi, g_lo = _split_bf16(g)
    bcum = _dot(tri, g_hi) + _dot(tri, g_lo)
    b_last = bcum[c - 1:c, :]
    rowk = lax.broadcasted_iota(I32, (c, dk), 0)
    lhs, rhs = [], []
    for j in range(nsub):
        b0 = bcum[j * sub:j * sub + 1, :]
        lhs.append(_bf(q * jnp.exp(jnp.minimum(bcum - b0, 0.0))))
        in_j = (rowk >= j * sub) & (rowk < (j + 1) * sub)
        rhs.append(_bf(jnp.where(in_j, k * jnp.exp(jnp.where(in_j, b0 - bcum, 0.0)), 0.0)))
    att = _dot_nt(jnp.concatenate(lhs, axis=1), jnp.concatenate(rhs, axis=1))
    att = jnp.where(row >= col, att, 0.0)
    v_bf = _bf(v)
    o = _dot(_bf(att), v_bf) + _dot(_bf(q * jnp.exp(bcum)), _bf(s))
    kd = _bf(k * jnp.exp(b_last - bcum))
    decay_col = jnp.transpose(jnp.broadcast_to(jnp.exp(b_last), (8, dk)))[:, 0:1]
    s_new = decay_col * s + _dot_tn(kd, v_bf)
    return o, s_new


def _gla_inputs(q_ref, k_ref, ga_ref, wg_ref, bg_ref):
    q = q_ref[...] * (GLA_DK ** -0.5)
    k = k_ref[...]
    x = _dot(_bf(ga_ref[...]), _bf(wg_ref[...])) + bg_ref[...]
    g = -_softplus(-x) / GLA_NORM
    return q, k, g


def _hgrn_inputs(q_ref, f_ref, lb_ref):
    q = _silu(q_ref[...]) * (HGRN_DK ** -0.5)
    lb = lb_ref[...]
    f = lb + (1.0 - lb) * _sigmoid(f_ref[...])
    return q, 1.0 - f, jnp.log(f)


def _scan_out(o, gain_ref, r_ref):
    return _bf(_rms(o, gain_ref[...]) * _silu(r_ref[...]))


SCAN_TB = 512


def _scan_prompt_body(*refs, kind):
    if kind == "gla":
        q_ref, k_ref, v_ref, r_ref, ga_ref, wg_ref, bg_ref, gain_ref, o_ref, st_ref, s_ref, obuf = refs
        q, k, g = _gla_inputs(q_ref, k_ref, ga_ref, wg_ref, bg_ref)
    else:
        q_ref, f_ref, v_ref, r_ref, lb_ref, gain_ref, o_ref, st_ref, s_ref, obuf = refs
        q, k, g = _hgrn_inputs(q_ref, f_ref, lb_ref)
    v = v_ref[...]
    tb = pl.program_id(2)

    @pl.when(tb == 0)
    def _():
        s_ref[...] = jnp.zeros_like(s_ref)

    cc = SCAN_CHUNK
    for ci in range(SCAN_TB // cc):
        sl = slice(ci * cc, (ci + 1) * cc)
        o, s_new = _scan_chunk(q[sl], k[sl], v[sl], g[sl], s_ref[...])
        s_ref[...] = s_new
        obuf[sl, :] = o
    o_ref[...] = _scan_out(obuf[...], gain_ref, r_ref)

    @pl.when(tb == pl.num_programs(2) - 1)
    def _():
        st_ref[...] = s_ref[...]


def scan_prompt_call(tok, proj, ga_proj, w_gate_pad, b_gate, gain, lb, kind):
    tb = SCAN_TB
    ntb = tok.seq // tb
    rows = lambda w, c0: pl.BlockSpec((tb, w), lambda b, h, t: (b * ntb + t, c0 // w + h))
    if kind == "gla":
        nh, dk, dv = GLA_HEADS, GLA_DK, GLA_DV
        in_specs = [rows(dk, 0), rows(dk, 512), rows(dv, 1024), rows(dv, 2048),
                    pl.BlockSpec((tb, 128), lambda b, h, t: (b * ntb + t, 0)),
                    pl.BlockSpec((128, dk), lambda b, h, t: (0, h)), pl.BlockSpec((1, dk), lambda b, h, t: (0, h)),
                    pl.BlockSpec((1, dv), lambda b, h, t: (0, 0))]
        args = (proj, proj, proj, proj, ga_proj, w_gate_pad, b_gate.reshape(1, -1), gain.reshape(1, -1))
    else:
        nh, dk, dv = HGRN_HEADS, HGRN_DK, HGRN_DV
        in_specs = [rows(dk, 3072), rows(dk, 4096), rows(dv, 5120), rows(dv, 6144),
                    pl.BlockSpec((1, dk), lambda b, h, t: (0, h)), pl.BlockSpec((1, dv), lambda b, h, t: (0, 0))]
        args = (proj, proj, proj, proj, lb.reshape(1, -1), gain.reshape(1, -1))
    return pl.pallas_call(
        functools.partial(_scan_prompt_body, kind=kind),
        grid=(tok.batch, nh, ntb),
        in_specs=in_specs,
        out_specs=[pl.BlockSpec((tb, dv), lambda b, h, t: (b * ntb + t, h)),
                   pl.BlockSpec((None, None, dk, dv), lambda b, h, t: (b, h, 0, 0))],
        out_shape=[SDS((tok.n_p, nh * dv), BF16), SDS((tok.batch, nh, dk, dv), F32)],
        scratch_shapes=[pltpu.VMEM((dk, dv), F32), pltpu.VMEM((tb, dv), F32)],
        compiler_params=_cp(("arbitrary", "arbitrary", "arbitrary"), 40),
        name="scan_prompt_" + kind,
    )(*args)


SCAN_SROWS = 8


def _scan_sample_body(*refs, kind, valid):
    if kind == "gla":
        p_ref, ga_ref, wg_ref, bg_ref, gain_ref, s0_ref, o_ref, st_ref = refs
        nh, dk, dv = GLA_HEADS, GLA_DK, GLA_DV
    else:
        p_ref, lb_ref, gain_ref, s0_ref, o_ref, st_ref = refs
        nh, dk, dv = HGRN_HEADS, HGRN_DK, HGRN_DV
    for h in range(nh):
        if kind == "gla":
            q = p_ref[:, h * dk:(h + 1) * dk] * (GLA_DK ** -0.5)
            k = p_ref[:, 512 + h * dk:512 + (h + 1) * dk]
            v = p_ref[:, 1024 + h * dv:1024 + (h + 1) * dv]
            r = p_ref[:, 2048 + h * dv:2048 + (h + 1) * dv]
            x = _dot(_bf(ga_ref[...]), _bf(wg_ref[:, h * dk:(h + 1) * dk])) + bg_ref[:, h * dk:(h + 1) * dk]
            g = -_softplus(-x) / GLA_NORM
        else:
            q = _silu(p_ref[:, 3072 + h * dk:3072 + (h + 1) * dk]) * (HGRN_DK ** -0.5)
            lb = lb_ref[:, h * dk:(h + 1) * dk]
            f = lb + (1.0 - lb) * _sigmoid(p_ref[:, 4096 + h * dk:4096 + (h + 1) * dk])
            k = 1.0 - f
            g = jnp.log(f)
            v = p_ref[:, 5120 + h * dv:5120 + (h + 1) * dv]
            r = p_ref[:, 6144 + h * dv:6144 + (h + 1) * dv]
        o, s_new = _scan_chunk(q, k, v, g, s0_ref[h], valid_rows=valid)
        st_ref[h] = s_new
        o_ref[:, h * dv:(h + 1) * dv] = _bf(_rms(o, gain_ref[...]) * _silu(r))


def scan_sample_call(proj_s, ga_s, w_gate_pad, b_gate, gain, lb, s0, kind, valid):
    db = proj_s.shape[0]
    r = SCAN_SROWS
    full = lambda shp: pl.BlockSpec(shp, lambda b: (0,) * len(shp))
    per_b = lambda shp: pl.BlockSpec((None,) + shp, lambda b: (b,) + (0,) * len(shp))
    if kind == "gla":
        nh, dk, dv = GLA_HEADS, GLA_DK, GLA_DV
        in_specs = [per_b((r, proj_s.shape[2])), per_b((r, 128)), full((128, nh * dk)), full((1, nh * dk)), full((1, dv)),
                    per_b((nh, dk, dv))]
        args = (proj_s, ga_s, w_gate_pad, b_gate.reshape(1, -1), gain.reshape(1, -1), s0)
    else:
        nh, dk, dv = HGRN_HEADS, HGRN_DK, HGRN_DV
        in_specs = [per_b((r, proj_s.shape[2])), full((1, nh * dk)), full((1, dv)), per_b((nh, dk, dv))]
        args = (proj_s, lb.reshape(1, -1), gain.reshape(1, -1), s0)
    return pl.pallas_call(
        functools.partial(_scan_sample_body, kind=kind, valid=valid),
        grid=(db,),
        in_specs=in_specs,
        out_specs=[per_b((r, nh * dv)), per_b((nh, dk, dv))],
        out_shape=[SDS((db, r, nh * dv), BF16), SDS((db, nh, dk, dv), F32)],
        compiler_params=_cp(("arbitrary",), 40),
        name="scan_sample_" + kind,
    )(*args)


def _moe_prep_body(x_ref, g_ref, scp_ref, scs_ref, shp_ref, shs_ref, wr_ref, br_ref,
                   hp_ref, sel_ref, idx_ref, wt_ref, *, p_tiles):
    is_s = pl.program_id(0) >= p_tiles
    y = _rms(x_ref[...], g_ref[...])
    sc = jnp.where(is_s, scs_ref[...], scp_ref[...])
    sh = jnp.where(is_s, shs_ref[...], shp_ref[...])
    h = y * (1.0 + sc) + sh
    h_hi, h_lo = _split_bf16(h)
    half = D_MODEL // 2
    hr = h_hi.astype(F32)
    lo_bits = lax.shift_right_logical(lax.bitcast_convert_type(hr[:, :half], U32), jnp.uint32(16))
    hi_bits = lax.bitcast_convert_type(hr[:, half:], U32)
    hp_ref[...] = hi_bits | lo_bits
    w_hi, w_lo = _split_bf16(wr_ref[...])
    logits = _dot(h_hi, w_hi) + _dot(h_lo, w_hi) + _dot(h_hi, w_lo) + br_ref[...]
    t, ne = logits.shape
    lane = lax.broadcasted_iota(I32, (t, ne), 1)
    lane4 = lax.broadcasted_iota(I32, (t, TOP_K), 1)
    vals = logits
    sel = jnp.zeros((t, ne), F32)
    idx4 = jnp.zeros((t, TOP_K), I32)
    e4 = jnp.zeros((t, TOP_K), F32)
    m0 = None
    for kk in range(TOP_K):
        m = jnp.max(vals, axis=1, keepdims=True)
        ik = jnp.min(jnp.where(vals == m, lane, ne), axis=1, keepdims=True)
        hot = lane == ik
        vals = jnp.where(hot, -jnp.inf, vals)
        sel = jnp.where(hot, 1.0, sel)
        if kk == 0:
            m0 = m
        idx4 = jnp.where(lane4 == kk, ik, idx4)
        e4 = jnp.where(lane4 == kk, jnp.exp(m - m0), e4)
    sel_ref[...] = sel.astype(sel_ref.dtype)
    idx_ref[...] = idx4
    wt_ref[...] = e4 / jnp.sum(e4, axis=1, keepdims=True)


def moe_prep_call(tok, x, g, mod_p, mod_s, w_router, b_router):
    d = D_MODEL
    row = lambda w: pl.BlockSpec((TM, w), lambda i: (i, 0))
    return pl.pallas_call(
        functools.partial(_moe_prep_body, p_tiles=tok.p_tiles),
        grid=(tok.tiles,),
        in_specs=[row(d), pl.BlockSpec((1, d), lambda i: (0, 0))] + tok.mod_specs(4, d) + tok.mod_specs(3, d)
        + [pl.BlockSpec((d, N_EXPERTS), lambda i: (0, 0)), pl.BlockSpec((1, N_EXPERTS), lambda i: (0, 0))],
        out_specs=[row(d // 2), row(N_EXPERTS), row(TOP_K), row(TOP_K)],
        out_shape=[SDS((tok.n, d // 2), U32), SDS((tok.n, N_EXPERTS), BF16), SDS((tok.n, TOP_K), I32),
                   SDS((tok.n, TOP_K), F32)],
        compiler_params=_cp(("arbitrary",), 48),
        name="moe_prep",
    )(x, g.reshape(1, d), mod_p, mod_s, mod_p, mod_s, w_router, b_router.reshape(1, -1))


RANK_TM = 256


def _moe_rank_body(sel_ref, rank_ref, cnt_ref):
    @pl.when(pl.program_id(0) == 0)
    def _():
        cnt_ref[...] = jnp.zeros_like(cnt_ref)

    sel = sel_ref[...]
    t = sel.shape[0]
    r = lax.broadcasted_iota(I32, (t, t), 0)
    c = lax.broadcasted_iota(I32, (t, t), 1)
    before = jnp.where(r > c, 1.0, 0.0).astype(BF16)
    base = cnt_ref[...]
    rank_ref[...] = base + _dot(before, sel)
    cnt_ref[...] = base + jnp.sum(sel.astype(F32), axis=0, keepdims=True)


def moe_rank_call(sel):
    n, ne = sel.shape
    t = RANK_TM
    return pl.pallas_call(
        _moe_rank_body,
        grid=(n // t,),
        in_specs=[pl.BlockSpec((t, ne), lambda i: (i, 0))],
        out_specs=[pl.BlockSpec((t, ne), lambda i: (i, 0)), pl.BlockSpec((1, ne), lambda i: (0, 0))],
        out_shape=[SDS((n, ne), F32), SDS((1, ne), F32)],
        compiler_params=_cp(("arbitrary",), 32),
        name="moe_rank",
    )(sel)


DISP_TM = 256


def _zero_fill(used, tail_ref, dst_hbm, zrows, sem):
    n_rows = dst_hbm.shape[0]
    zrows[...] = jnp.zeros_like(zrows)

    def zcopy(row):
        return pltpu.make_async_copy(zrows, dst_hbm.at[pl.ds(pl.multiple_of(row, ROW_ALIGN), ROW_ALIGN)], sem)

    for act in ("start", "wait"):
        if tail_ref is not None:
            for e in range(N_EXPERTS):
                @pl.when(tail_ref[e] >= 0)
                def _(e=e, act=act):
                    getattr(zcopy(tail_ref[e]), act)()

        for r in range(N_EXPERTS):
            @pl.when(used + r * ROW_ALIGN < n_rows)
            def _(r=r, act=act):
                getattr(zcopy(used + r * ROW_ALIGN), act)()


def _moe_dispatch_body(tail_ref, pos_ref, h_ref, xs_hbm, zrows, sem):
    i = pl.program_id(0)
    t = DISP_TM

    @pl.when(i == 0)
    def _():
        _zero_fill(tail_ref[N_EXPERTS], tail_ref, xs_hbm, zrows, sem)

    def copy(tt, kk):
        return pltpu.make_async_copy(h_ref.at[pl.ds(tt, 1)], xs_hbm.at[pl.ds(pos_ref[tt * TOP_K + kk], 1)], sem)

    def issue(tt, c):
        for kk in range(TOP_K):
            copy(tt, kk).start()
        return c

    def drain(tt, c):
        for kk in range(TOP_K):
            copy(tt, kk).wait()
        return c

    lax.fori_loop(0, t, issue, 0)
    lax.fori_loop(0, t, drain, 0)


def moe_dispatch_call(tail_rows, pos_flat, hp, n_rows):
    n, w = hp.shape
    t = DISP_TM
    grid_spec = pltpu.PrefetchScalarGridSpec(
        num_scalar_prefetch=1,
        grid=(n // t,),
        in_specs=[pl.BlockSpec((t * TOP_K,), lambda i, tl: (i,), memory_space=pltpu.SMEM),
                  pl.BlockSpec((t, w), lambda i, tl: (i, 0))],
        out_specs=pl.BlockSpec(memory_space=pl.ANY),
        scratch_shapes=[pltpu.VMEM((ROW_ALIGN, w), U32), pltpu.SemaphoreType.DMA(())],
    )
    return pl.pallas_call(
        _moe_dispatch_body,
        grid_spec=grid_spec,
        out_shape=SDS((n_rows, w), U32),
        compiler_params=_cp(("arbitrary",), 32),
        name="moe_dispatch",
    )(tail_rows, pos_flat, hp)


def _row_copies(src, src_off, dst, dst_off, nrows, sem, chunk, act):
    nfull = nrows // chunk

    def body(j, c):
        o = j * chunk
        act(pltpu.make_async_copy(src.at[pl.ds(pl.multiple_of(src_off + o, ROW_ALIGN), chunk)],
                                  dst.at[pl.ds(pl.multiple_of(dst_off + o, ROW_ALIGN), chunk)], sem))
        return c

    lax.fori_loop(0, nfull, body, 0)
    rem = nrows - nfull * chunk
    done = nfull * chunk
    bit = chunk // 2
    while bit >= ROW_ALIGN:
        take = (rem & bit) != 0

        @pl.when(take)
        def _(done=done, bit=bit):
            act(pltpu.make_async_copy(src.at[pl.ds(pl.multiple_of(src_off + done, ROW_ALIGN), bit)],
                                      dst.at[pl.ds(pl.multiple_of(dst_off + done, ROW_ALIGN), bit)], sem))

        done = done + jnp.where(take, bit, 0)
        bit //= 2


def _moe_expert_body(ie_ref, if_ref, start_ref, rows_ref, used_ref,
                     xs_hbm, wg_ref, wl_ref, bg_ref, bl_ref, wd_ref, bd_ref, ys_hbm,
                     xin, yacc, zrows, sem):
    it = pl.program_id(0)
    f = pl.program_id(1)
    nf = pl.num_programs(1)
    rows = rows_ref[it]
    start = start_ref[it]
    half = D_MODEL // 2
    sub = MOE_SUB
    hsub = sub // 2
    n_full = rows // sub
    rem = rows - n_full * sub
    n_big = n_full + jnp.where(rem > hsub, 1, 0)
    has_small = jnp.logical_and(rem > 0, rem <= hsub)

    @pl.when(jnp.logical_and(it == 0, f == 0))
    def _():
        xin[...] = jnp.zeros_like(xin)

    def wait_out(item):
        _row_copies(yacc, 0, ys_hbm, start_ref[item], rows_ref[item], sem.at[1], sub, lambda cp: cp.wait())

    prev = jnp.maximum(it - 1, 0)

    @pl.when(jnp.logical_and(f == 0, jnp.logical_and(it > 0, rows_ref[prev] > 0)))
    def _():
        wait_out(prev)

    @pl.when(rows > 0)
    def _():
        @pl.when(f == 0)
        def _():
            _row_copies(xs_hbm, start, xin, 0, rows, sem.at[0], sub, lambda cp: cp.start())

            def init(r, c):
                yacc[pl.ds(pl.multiple_of(r * hsub, hsub), hsub), :] = jnp.broadcast_to(bd_ref[...], (hsub, D_MODEL))
                return c

            lax.fori_loop(0, (rows + hsub - 1) // hsub, init, 0)
            _row_copies(xs_hbm, start, xin, 0, rows, sem.at[0], sub, lambda cp: cp.wait())

        def tile(o, n):
            w = xin[pl.ds(o, n), :]
            x_lo = lax.bitcast_convert_type(lax.shift_left(w, jnp.uint32(16)), F32)
            x_hi = lax.bitcast_convert_type(w & jnp.uint32(0xFFFF0000), F32)
            gate = _dot(x_lo, wg_ref[0:half, :]) + _dot(x_hi, wg_ref[half:, :]) + bg_ref[...]
            lin = _dot(x_lo, wl_ref[0:half, :]) + _dot(x_hi, wl_ref[half:, :]) + bl_ref[...]
            gate = jnp.minimum(gate, SWIGLU_LIMIT)
            lin = jnp.clip(lin, -SWIGLU_LIMIT, SWIGLU_LIMIT)
            act = (lin + 1.0) * gate * _sigmoid(SWIGLU_ALPHA * gate)
            yacc[pl.ds(o, n), :] += _dot(act, wd_ref[...])

        def pair(p, c):
            o = pl.multiple_of(p * (2 * sub), 2 * sub)
            tile(o, sub)
            tile(o + sub, sub)
            return c

        lax.fori_loop(0, n_big // 2, pair, 0)

        @pl.when(n_big % 2 == 1)
        def _():
            tile(pl.multiple_of((n_big - 1) * sub, sub), sub)

        @pl.when(has_small)
        def _():
            tile(pl.multiple_of(n_full * sub, sub), hsub)

        @pl.when(f == nf - 1)
        def _():
            _row_copies(yacc, 0, ys_hbm, start, rows, sem.at[1], sub, lambda cp: cp.start())

    @pl.when(jnp.logical_and(it == pl.num_programs(0) - 1, f == nf - 1))
    def _():
        @pl.when(rows > 0)
        def _():
            wait_out(it)

        _zero_fill(used_ref[0], None, ys_hbm, zrows, sem.at[1])


def moe_expert_call(layer, item_e, item_f, item_start, item_rows, used, xs, w_gate_up, b_gate_up, w_down, b_down):
    nl, ne, d, ff2 = w_gate_up.shape
    ff = ff2 // 2
    nf = ff // MOE_F
    n_items = item_e.shape[0]
    n_rows = xs.shape[0]

    def wmap(col0, kind):
        def index_map(i, f, ie, ife, st, rw, us):
            fi = jnp.where(ife[i] >= 0, f, nf - 1)
            if kind == "col":
                return layer, ie[i], 0, col0 + fi
            if kind == "row":
                return layer, ie[i], fi, 0
            return layer, ie[i], 0, 0
        return index_map

    grid_spec = pltpu.PrefetchScalarGridSpec(
        num_scalar_prefetch=5,
        grid=(n_items, nf),
        in_specs=[pl.BlockSpec(memory_space=pl.ANY),
                  pl.BlockSpec((None, None, d, MOE_F), wmap(0, "col")),
                  pl.BlockSpec((None, None, d, MOE_F), wmap(nf, "col")),
                  pl.BlockSpec((None, None, 1, MOE_F), wmap(0, "col")),
                  pl.BlockSpec((None, None, 1, MOE_F), wmap(nf, "col")),
                  pl.BlockSpec((None, None, MOE_F, d), wmap(0, "row")),
                  pl.BlockSpec((None, None, 1, d), wmap(0, "fixed"))],
        out_specs=pl.BlockSpec(memory_space=pl.ANY),
        scratch_shapes=[pltpu.VMEM((MOE_ROWS, d // 2), U32), pltpu.VMEM((MOE_ROWS, d), F32),
                        pltpu.VMEM((ROW_ALIGN, d), F32), pltpu.SemaphoreType.DMA((2,))],
    )
    return pl.pallas_call(
        _moe_expert_body,
        grid_spec=grid_spec,
        out_shape=SDS((n_rows, d), F32),
        compiler_params=_cp(("arbitrary", "arbitrary"), MOE_VMEM_MIB),
        name="moe_experts",
    )(item_e, item_f, item_start, item_rows, used, xs, w_gate_up, w_gate_up,
      b_gate_up.reshape(nl, ne, 1, ff2), b_gate_up.reshape(nl, ne, 1, ff2), w_down, b_down.reshape(nl, ne, 1, d))


COMB_TM = 256


def _moe_combine_body(pos_ref, ys_hbm, x_ref, wt_ref, gp_ref, gs_ref, gf_ref, *rest, p_tiles, tpb, final):
    if final:
        yp_ref, ysm_ref, buf, sem = rest
    else:
        o_ref, buf, sem = rest
    i = pl.program_id(0)
    t = COMB_TM

    def copy(tt, kk):
        return pltpu.make_async_copy(ys_hbm.at[pl.ds(pos_ref[tt * TOP_K + kk], 1)], buf.at[kk, pl.ds(tt, 1)], sem)

    def issue(tt, c):
        for kk in range(TOP_K):
            copy(tt, kk).start()
        return c

    def drain(tt, c):
        for kk in range(TOP_K):
            copy(tt, kk).wait()
        return c

    lax.fori_loop(0, t, issue, 0)
    lax.fori_loop(0, t, drain, 0)
    wt = wt_ref[...]
    moe = wt[:, 0:1] * buf[0]
    for kk in range(1, TOP_K):
        moe = moe + wt[:, kk:kk + 1] * buf[kk]
    is_s = i >= p_tiles * tpb
    gate = jnp.where(is_s, gs_ref[...], gp_ref[...])
    x = x_ref[...] + gate * moe
    if final:
        @pl.when(jnp.logical_not(is_s))
        def _():
            yp_ref[...] = _rms(x, gf_ref[...])

        @pl.when(is_s)
        def _():
            ysm_ref[...] = _rms(x, gf_ref[...])
    else:
        o_ref[...] = x


def moe_combine_call(tok, pos_flat, ys, x, wt, mod_p, mod_s, g_final, final):
    d = D_MODEL
    t = COMB_TM
    tpb = TM // t
    np_t = tok.p_tiles * tpb
    row = lambda w: pl.BlockSpec((t, w), lambda i: (i, 0))
    if final:
        out_specs = [pl.BlockSpec((t, d), lambda i: (jnp.minimum(i, np_t - 1), 0)),
                     pl.BlockSpec((t, d), lambda i: (jnp.maximum(i - np_t, 0), 0))]
        out_shape = [SDS((tok.n_p, d), F32), SDS((tok.n_s, d), F32)]
    else:
        out_specs = [row(d)]
        out_shape = [SDS((tok.n, d), F32)]

    def pmap(i):
        return jnp.minimum(i // (tok.tiles_per_batch * tpb), tok.batch - 1), 0, 5

    def smap(i):
        return jnp.maximum(i - tok.p_tiles * tpb, 0), 5

    return pl.pallas_call(
        functools.partial(_moe_combine_body, p_tiles=tok.p_tiles, tpb=tpb, final=final),
        grid=(tok.n // t,),
        in_specs=[pl.BlockSpec((t * TOP_K,), lambda i: (i,), memory_space=pltpu.SMEM),
                  pl.BlockSpec(memory_space=pl.ANY), row(d), row(TOP_K),
                  pl.BlockSpec((None, 1, d), pmap), pl.BlockSpec((t, d), smap),
                  pl.BlockSpec((1, d), lambda i: (0, 0))],
        out_specs=out_specs,
        out_shape=out_shape,
        scratch_shapes=[pltpu.VMEM((TOP_K, t, d), F32), pltpu.SemaphoreType.DMA(())],
        compiler_params=_cp(("arbitrary",), 48),
        name="moe_combine",
    )(pos_flat, ys, x, wt, mod_p, mod_s, g_final.reshape(1, d))


def moe_layer(tok, layer, x, g, mod_p, mod_s, w_router, b_router, w_gate_up, b_gate_up, w_down, b_down, g_final, final):
    hp, sel, idx4, wt = moe_prep_call(tok, x, g, mod_p, mod_s, w_router[layer], b_router[layer])
    rank, cnt = moe_rank_call(sel)
    real = cnt[0].astype(I32)
    counts = (real + ROW_ALIGN - 1) // ROW_ALIGN * ROW_ALIGN
    offs = jnp.cumsum(counts) - counts
    pos = jnp.take_along_axis(rank.astype(I32) + offs[None, :], idx4, axis=1)
    pos_flat = pos.reshape(-1)
    tail_rows = jnp.where(real > 0, offs + counts - ROW_ALIGN, -1).astype(I32)
    n_rows = tok.n * TOP_K + N_EXPERTS * ROW_ALIGN
    n_items = N_EXPERTS + n_rows // MOE_ROWS
    passes = (counts + MOE_ROWS - 1) // MOE_ROWS
    pend = jnp.cumsum(passes)
    total = pend[-1]
    ii = jnp.arange(n_items, dtype=I32)
    e_of = jnp.minimum(jnp.sum((pend[None, :] <= ii[:, None]).astype(I32), axis=1), N_EXPERTS - 1)
    j_of = ii - (pend - passes)[e_of]
    valid = ii < total
    last_e = e_of[jnp.maximum(total - 1, 0)]
    item_e = jnp.where(valid, e_of, last_e)
    item_f = jnp.where(valid, 0, -1).astype(I32)
    item_start = jnp.where(valid, offs[e_of] + j_of * MOE_ROWS, 0).astype(I32)
    item_rows = jnp.where(valid, jnp.minimum(counts[e_of] - j_of * MOE_ROWS, MOE_ROWS), 0).astype(I32)
    used = jnp.sum(counts).astype(I32).reshape(1)
    xs = moe_dispatch_call(jnp.concatenate([tail_rows, used]), pos_flat, hp, n_rows)
    ys = moe_expert_call(layer, item_e, item_f, item_start, item_rows, used, xs, w_gate_up, b_gate_up, w_down, b_down)
    return moe_combine_call(tok, pos_flat, ys, x, wt, mod_p, mod_s, g_final, final)


def _rope_tables(pos):
    half = MLA_ROPE // 2
    inv = ROPE_THETA ** (-jnp.arange(half, dtype=F32) / half)
    ang = pos.astype(F32)[:, None] * inv
    cos, sin = jnp.cos(ang), jnp.sin(ang)
    z = jnp.zeros((pos.shape[0], 128 - MLA_ROPE), F32)
    return jnp.concatenate([cos, cos, z], axis=1), jnp.concatenate([-sin, sin, z], axis=1)


def _swap_halves(w):
    half = w.shape[-1] // 2
    return jnp.concatenate([w[..., half:], w[..., :half]], axis=-1)


def kernel(x_prompt, x_sample, cache_sb_k, cache_sb_v, cache_mla_latent, cache_mla_krope, state_gla, state_hgrn, page_table, c_prompt, c_sample, w_ada, b_ada, g_norm, w_in_ab, g_q_a, w_q_b, g_kv_a, w_kv_b, w_out_ab, w_in_cd, w_gla_gate, b_gla_gate, g_gla_out, hgrn_lb, g_hgrn_out, w_out_cd, w_router, b_router, w_gate_up, b_gate_up, w_down, b_down, g_final):
    batch, seq, d = x_prompt.shape
    db, ds, _ = x_sample.shape
    depth = w_ada.shape[0]
    past_len = page_table.shape[1] * PAGE
    tok = Tok(batch, seq, db, ds)
    assert d == D_MODEL and ds <= DEC_QROWS // SB_GROUP

    x = jnp.concatenate([x_prompt.reshape(tok.n_p, d), x_sample.reshape(tok.n_s, d)], axis=0)
    n_c = batch + db
    c_all = jnp.concatenate([c_prompt, c_sample, jnp.zeros((-n_c % 8, d), F32)], axis=0)
    mod = adaln_call(c_all, w_ada, b_ada)

    pos_all = jnp.concatenate([jnp.tile(jnp.arange(seq, dtype=I32), batch),
                               jnp.tile(past_len + jnp.arange(ds, dtype=I32), db)])
    cos_t, sin_t = _rope_tables(pos_all)
    lb_soft = jax.nn.softmax(hgrn_lb.astype(F32), axis=0)
    lower_bounds = jnp.cumsum(lb_soft, axis=0) - lb_soft[0]

    outs_ab, outs_cd = [], []
    y_final = None
    for l in range(depth):
        j = l // 2
        mod_p = mod[l, :batch].reshape(batch, 1, 6 * d)
        mod_s = jnp.repeat(mod[l, batch:n_c], ds, axis=0)
        h = normmod_call(tok, x, g_norm[l, 0], mod_p, mod_s, 0, 1)
        if l % 2 == 0:
            w_in = w_in_ab[j]
            zc = jnp.zeros((d, 128 - MLA_ROPE), F32)
            w_kr = w_in[:, 2816:2880]
            w_aug = jnp.concatenate([w_in[:, :2816], w_kr, zc, _swap_halves(w_kr), zc], axis=1)
            proj = matmul_call(h, w_aug, 1024, "in_proj_ab")
            wq = w_q_b[j].reshape(MLA_QL, MLA_HEADS, MLA_NOPE + MLA_ROPE)
            zq = jnp.zeros((MLA_QL, MLA_HEADS, 128 - MLA_ROPE), F32)
            wq_rope = wq[:, :, MLA_NOPE:]
            wq_aug = jnp.concatenate([wq[:, :, :MLA_NOPE].reshape(MLA_QL, -1),
                                      jnp.concatenate([wq_rope, zq], axis=2).reshape(MLA_QL, -1),
                                      jnp.concatenate([_swap_halves(wq_rope), zq], axis=2).reshape(MLA_QL, -1)], axis=1)
            wk_abs = jnp.transpose(w_kv_b[j][:, :, :MLA_NOPE], (1, 2, 0))
            wv = jnp.transpose(w_kv_b[j][:, :, MLA_NOPE:], (1, 0, 2))
            sbq, sbk, sbv, sbk_b, sbv_b, lat, krope, qm, km = ab_post_call(proj, cos_t, sin_t, g_q_a[j], g_kv_a[j], wq_aug, wk_abs)
            o_sb_p = sb_prompt_call(tok, sbq, sbk_b, sbv_b)
            o_ml_p = mla_prompt_call(tok, qm, km, wv)
            qsb_d = sbq[tok.n_p:].reshape(db, ds, SB_KV, SB_GROUP, SB_DIM).transpose(0, 2, 3, 1, 4)
            qsb_d = jnp.pad(qsb_d, ((0, 0), (0, 0), (0, 0), (0, DEC_QROWS // SB_GROUP - ds), (0, 0)))
            qsb_d = qsb_d.reshape(db, SB_KV * DEC_QROWS, SB_DIM)
            qml_d = qm[tok.n_p:].reshape(db, ds, MLA_HEADS, MLA_QW).transpose(0, 2, 1, 3)
            qml_d = jnp.pad(qml_d, ((0, 0), (0, 0), (0, DEC_QROWS - ds), (0, 0))).reshape(db, MLA_HEADS * DEC_QROWS, MLA_QW)
            padk = lambda a: jnp.pad(a[tok.n_p:].reshape(db, ds, -1), ((0, 0), (0, PAGE - ds), (0, 0)))
            rows_th = lambda a: padk(a).reshape(db, PAGE * SB_KV, SB_DIM)
            o_sb_s, o_ml_s = decode_call(page_table, qsb_d, qml_d, rows_th(sbk), rows_th(sbv), padk(lat),
                                         jnp.swapaxes(padk(krope), 1, 2), wv,
                                         cache_sb_k, cache_sb_v, cache_mla_latent, jnp.swapaxes(cache_mla_krope, 2, 3), j)
            hs = DEC_QROWS // SB_GROUP
            o_sb_s = o_sb_s.reshape(db, SB_KV, SB_GROUP, hs, SB_DIM)[:, :, :, :ds].transpose(0, 3, 1, 2, 4).reshape(tok.n_s, -1)
            o_ml_s = o_ml_s[:, :ds].reshape(tok.n_s, -1)
            o = jnp.concatenate([jnp.concatenate([o_sb_p, o_ml_p], axis=1), jnp.concatenate([o_sb_s, o_ml_s], axis=1)], axis=0)
            x = matmul_res_call(tok, o, w_out_ab[j], x, mod_p, mod_s, 2, "out_proj_ab")
            outs_ab.append((sbk, sbv, lat, krope))
        else:
            w_in = w_in_cd[j]
            w_main = jnp.concatenate([w_in[:, :2048], w_in[:, 2064:]], axis=1)
            w_ga = jnp.concatenate([w_in[:, 2048:2064], jnp.zeros((d, 128 - GLA_RANK), F32)], axis=1)
            proj = matmul_call(h, w_main, 1024, "in_proj_cd")
            ga = matmul_call(h, w_ga, 128, "in_proj_gate")
            w_gate_pad = jnp.concatenate([w_gla_gate[j], jnp.zeros((128 - GLA_RANK, GLA_HEADS * GLA_DK), F32)], axis=0)
            lb = lower_bounds[l]
            o_gla_p, st_gla_p = scan_prompt_call(tok, proj, ga, w_gate_pad, b_gla_gate[j], g_gla_out[j], lb, "gla")
            o_hgrn_p, st_hgrn_p = scan_prompt_call(tok, proj, ga, w_gate_pad, b_gla_gate[j], g_hgrn_out[j], lb, "hgrn")
            pad_t = lambda a: jnp.pad(a[tok.n_p:].reshape(db, ds, -1), ((0, 0), (0, SCAN_SROWS - ds), (0, 0)))
            proj_s, ga_s = pad_t(proj), pad_t(ga)
            o_gla_s, st_gla_s = scan_sample_call(proj_s, ga_s, w_gate_pad, b_gla_gate[j], g_gla_out[j], lb, state_gla[j], "gla", ds)
            o_hgrn_s, st_hgrn_s = scan_sample_call(proj_s, ga_s, w_gate_pad, b_gla_gate[j], g_hgrn_out[j], lb, state_hgrn[j], "hgrn", ds)
            o_s = jnp.concatenate([o_gla_s[:, :ds].reshape(tok.n_s, -1), o_hgrn_s[:, :ds].reshape(tok.n_s, -1)], axis=1)
            o = jnp.concatenate([jnp.concatenate([o_gla_p, o_hgrn_p], axis=1), o_s], axis=0)
            x = matmul_res_call(tok, o, w_out_cd[j], x, mod_p, mod_s, 2, "out_proj_cd")
            outs_cd.append((st_gla_p, st_hgrn_p, st_gla_s, st_hgrn_s))
        res = moe_layer(tok, l, x, g_norm[l, 1], mod_p, mod_s, w_router, b_router, w_gate_up, b_gate_up,
                        w_down, b_down, g_final, l == depth - 1)
        if l == depth - 1:
            y_final = res
        else:
            x = res[0]

    y_prompt = y_final[0].reshape(batch, seq, d)
    y_sample = y_final[1].reshape(db, ds, d)

    def grp(a, s, shape):
        return a[s].reshape(shape)

    sp, ss = slice(0, tok.n_p), slice(tok.n_p, tok.n)
    sb_k_p = jnp.stack([grp(r[0], sp, (batch, seq, SB_KV, SB_DIM)) for r in outs_ab])
    sb_v_p = jnp.stack([grp(r[1], sp, (batch, seq, SB_KV, SB_DIM)) for r in outs_ab])
    lat_p = jnp.stack([grp(r[2], sp, (batch, seq, MLA_KVL)) for r in outs_ab])
    krope_p = jnp.stack([grp(r[3], sp, (batch, seq, MLA_ROPE)) for r in outs_ab])
    sb_k_s = jnp.stack([grp(r[0], ss, (db, ds, SB_KV, SB_DIM)) for r in outs_ab])
    sb_v_s = jnp.stack([grp(r[1], ss, (db, ds, SB_KV, SB_DIM)) for r in outs_ab])
    lat_s = jnp.stack([grp(r[2], ss, (db, ds, MLA_KVL)) for r in outs_ab])
    krope_s = jnp.stack([grp(r[3], ss, (db, ds, MLA_ROPE)) for r in outs_ab])
    gla_p = jnp.stack([r[0] for r in outs_cd])
    hgrn_p = jnp.stack([r[1] for r in outs_cd])
    gla_s = jnp.stack([r[2] for r in outs_cd])
    hgrn_s = jnp.stack([r[3] for r in outs_cd])
    return (y_prompt, y_sample, sb_k_p, sb_v_p, lat_p, krope_p, gla_p, hgrn_p,
            sb_k_s, sb_v_s, lat_s, krope_s, gla_s, hgrn_s)
```

```python
import functools
import math

import numpy as np
import jax
import jax.numpy as jnp
from jax import lax
from jax.experimental import pallas as pl
from jax.experimental.pallas import tpu as pltpu

F32 = jnp.float32
BF16 = jnp.bfloat16
I32 = jnp.int32
U32 = jnp.uint32
SDS = jax.ShapeDtypeStruct

D_MODEL = 2048
EPS = 1e-6
PAGE = 128
SB_HEADS, SB_KV, SB_GROUP, SB_DIM = 8, 4, 2, 128
MLA_HEADS, MLA_QL, MLA_KVL, MLA_NOPE, MLA_ROPE, MLA_V = 8, 512, 256, 128, 64, 128
MLA_SCALE = (MLA_NOPE + MLA_ROPE) ** -0.5
SB_SCALE = SB_DIM ** -0.5
ROPE_THETA = 10000.0
MLA_QW = MLA_KVL + 128
GLA_HEADS, GLA_DK, GLA_DV, GLA_RANK, GLA_NORM = 4, 128, 256, 16, 16.0
HGRN_HEADS, HGRN_DK, HGRN_DV = 8, 128, 128
SCAN_CHUNK = 64
SCAN_SUB = 16
N_EXPERTS, TOP_K = 32, 4
SWIGLU_LIMIT, SWIGLU_ALPHA = 7.0, 1.702
NEG = -1e30

TM = 512
MOE_ROWS = 1280
MOE_SUB = 256
MOE_F = 512
MOE_VMEM_MIB = 58
ROW_ALIGN = 8
MIB = 1 << 20


def _cp(sem, vmem_mib):
    return pltpu.CompilerParams(dimension_semantics=sem, vmem_limit_bytes=vmem_mib * MIB)


def _bf(x):
    return x.astype(BF16)


def _dot(a, b):
    return jnp.dot(a, b, preferred_element_type=F32)


def _dot_nt(a, b):
    return lax.dot_general(a, b, (((1,), (1,)), ((), ())), preferred_element_type=F32)


def _dot_tn(a, b):
    return lax.dot_general(a, b, (((0,), (0,)), ((), ())), preferred_element_type=F32)


def _sigmoid(x):
    return 1.0 / (1.0 + jnp.exp(-x))


def _silu(x):
    return x * _sigmoid(x)


def _softplus(x):
    return jnp.maximum(x, 0.0) + jnp.log(1.0 + jnp.exp(-jnp.abs(x)))


def _rms(x, g):
    return x * lax.rsqrt(jnp.mean(x * x, axis=-1, keepdims=True) + EPS) * g


def _split_bf16(x):
    hi = _bf(x)
    lo = _bf(x - hi.astype(F32))
    return hi, lo


class Tok:
    def __init__(self, batch, seq, dec_batch, dec_seq):
        self.batch, self.seq, self.dec_batch, self.dec_seq = batch, seq, dec_batch, dec_seq
        self.n_p = batch * seq
        self.n_s = dec_batch * dec_seq
        self.n = self.n_p + self.n_s
        assert seq % TM == 0 and self.n_s % TM == 0
        self.tiles_per_batch = seq // TM
        self.p_tiles = self.n_p // TM
        self.tiles = self.n // TM

    def mod_specs(self, k, width, col_of=None, grid_axis=0, ngrid=1):
        nblk = D_MODEL // width

        def pick(args):
            i = args[grid_axis]
            c = k * nblk + (col_of(*args) if col_of is not None else 0)
            return i, c

        def pmap(*args):
            i, c = pick(args)
            return jnp.minimum(i // self.tiles_per_batch, self.batch - 1), 0, c

        def smap(*args):
            i, c = pick(args)
            return jnp.maximum(i - self.p_tiles, 0), c

        return [pl.BlockSpec((None, 1, width), pmap), pl.BlockSpec((TM, width), smap)]


def _adaln_body(c_ref, w_ref, b_ref, o_ref):
    s = _bf(_silu(c_ref[...]))
    o_ref[...] = _dot(s, _bf(w_ref[...])) + b_ref[...]


def adaln_call(c_all, w_ada, b_ada):
    r = c_all.shape[0]
    nl, d, w6 = w_ada.shape
    tn = 1024
    return pl.pallas_call(
        _adaln_body,
        grid=(nl, w6 // tn),
        in_specs=[pl.BlockSpec((r, d), lambda l, n: (0, 0)),
                  pl.BlockSpec((None, d, tn), lambda l, n: (l, 0, n)),
                  pl.BlockSpec((None, 1, tn), lambda l, n: (l, 0, n))],
        out_specs=pl.BlockSpec((None, r, tn), lambda l, n: (l, 0, n)),
        out_shape=SDS((nl, r, w6), F32),
        compiler_params=_cp(("arbitrary", "arbitrary"), 40),
        name="adaln",
    )(c_all, w_ada, b_ada.reshape(nl, 1, w6))


def _normmod_body(x_ref, g_ref, scp_ref, scs_ref, shp_ref, shs_ref, o_ref, *, p_tiles):
    is_s = pl.program_id(0) >= p_tiles
    y = _rms(x_ref[...], g_ref[...])
    sc = jnp.where(is_s, scs_ref[...], scp_ref[...])
    sh = jnp.where(is_s, shs_ref[...], shp_ref[...])
    o_ref[...] = (y * (1.0 + sc) + sh).astype(o_ref.dtype)


def normmod_call(tok, x, g, mod_p, mod_s, k_shift, k_scale):
    d = D_MODEL
    return pl.pallas_call(
        functools.partial(_normmod_body, p_tiles=tok.p_tiles),
        grid=(tok.tiles,),
        in_specs=[pl.BlockSpec((TM, d), lambda i: (i, 0)), pl.BlockSpec((1, d), lambda i: (0, 0))]
        + tok.mod_specs(k_scale, d) + tok.mod_specs(k_shift, d),
        out_specs=pl.BlockSpec((TM, d), lambda i: (i, 0)),
        out_shape=SDS((tok.n, d), BF16),
        compiler_params=_cp(("arbitrary",), 56),
        name="normmod",
    )(x, g.reshape(1, d), mod_p, mod_s, mod_p, mod_s)


def _mm_body(a_ref, w_ref, o_ref, wbf_ref):
    @pl.when(pl.program_id(1) == 0)
    def _():
        wbf_ref[...] = _bf(w_ref[...])

    o_ref[...] = _dot(a_ref[...], wbf_ref[...])


def _mm_res_body(a_ref, w_ref, r_ref, gp_ref, gs_ref, o_ref, wbf_ref, *, p_tiles):
    @pl.when(pl.program_id(1) == 0)
    def _():
        wbf_ref[...] = _bf(w_ref[...])

    gate = jnp.where(pl.program_id(1) >= p_tiles, gs_ref[...], gp_ref[...])
    o_ref[...] = r_ref[...] + gate * _dot(a_ref[...], wbf_ref[...])


def matmul_call(a, w, tn, name):
    n, k = a.shape
    nout = w.shape[1]
    assert n % TM == 0 and nout % tn == 0
    return pl.pallas_call(
        _mm_body,
        grid=(nout // tn, n // TM),
        in_specs=[pl.BlockSpec((TM, k), lambda j, i: (i, 0)), pl.BlockSpec((k, tn), lambda j, i: (0, j))],
        out_specs=pl.BlockSpec((TM, tn), lambda j, i: (i, j)),
        out_shape=SDS((n, nout), F32),
        scratch_shapes=[pltpu.VMEM((k, tn), BF16)],
        compiler_params=_cp(("arbitrary", "arbitrary"), 48),
        name=name,
    )(a, w)


def matmul_res_call(tok, a, w, res, mod_p, mod_s, k_gate, name):
    n, k = a.shape
    nout = w.shape[1]
    tn = 1024
    return pl.pallas_call(
        functools.partial(_mm_res_body, p_tiles=tok.p_tiles),
        grid=(nout // tn, n // TM),
        in_specs=[pl.BlockSpec((TM, k), lambda j, i: (i, 0)), pl.BlockSpec((k, tn), lambda j, i: (0, j)),
                  pl.BlockSpec((TM, tn), lambda j, i: (i, j))]
        + tok.mod_specs(k_gate, tn, col_of=lambda j, i: j, grid_axis=1),
        out_specs=pl.BlockSpec((TM, tn), lambda j, i: (i, j)),
        out_shape=SDS((n, nout), F32),
        scratch_shapes=[pltpu.VMEM((k, tn), BF16)],
        compiler_params=_cp(("arbitrary", "arbitrary"), 48),
        name=name,
    )(a, w, res, mod_p, mod_s)


AB_TM = 256


def _ab_post_body(p_ref, cos_ref, sin_ref, gq_ref, gkv_ref, wq_ref, wk_ref,
                  sbq_ref, sbk_ref, sbv_ref, sbkb_ref, sbvb_ref, lat_ref, kr_ref, qm_ref, km_ref,
                  wqb_ref, wkb_ref):
    @pl.when(pl.program_id(0) == 0)
    def _():
        wqb_ref[...] = _bf(wq_ref[...])
        wkb_ref[...] = _bf(wk_ref[...])

    cos = cos_ref[...]
    sin = sin_ref[...]
    sbq_ref[...] = _bf(p_ref[:, 0:1024])
    sbk = p_ref[:, 1024:1536]
    sbv = p_ref[:, 1536:2048]
    sbk_ref[...] = sbk
    sbv_ref[...] = sbv
    sbkb_ref[...] = _bf(sbk)
    sbvb_ref[...] = _bf(sbv)
    lat = _rms(p_ref[:, 2560:2816], gkv_ref[...])
    lat_ref[...] = lat
    kr = p_ref[:, 2816:2944] * cos + p_ref[:, 2944:3072] * sin
    kr_ref[...] = kr[:, :MLA_ROPE]
    km_ref[:, 0:MLA_KVL] = _bf(lat)
    km_ref[:, MLA_KVL:MLA_QW] = _bf(kr)
    qn = _bf(_rms(p_ref[:, 2048:2560], gq_ref[...]))
    q2 = _dot(qn, wqb_ref[...])
    for h in range(MLA_HEADS):
        qlat = _dot(_bf(q2[:, h * 128:(h + 1) * 128]), wkb_ref[h])
        qr = q2[:, 1024 + h * 128:1024 + (h + 1) * 128] * cos + q2[:, 2048 + h * 128:2048 + (h + 1) * 128] * sin
        qm_ref[:, h * MLA_QW:h * MLA_QW + MLA_KVL] = _bf(qlat)
        qm_ref[:, h * MLA_QW + MLA_KVL:(h + 1) * MLA_QW] = _bf(qr)


def ab_post_call(proj, cos_t, sin_t, g_q_a, g_kv_a, wq_aug, wk_abs):
    n = proj.shape[0]
    t = AB_TM
    row = lambda w: pl.BlockSpec((t, w), lambda i: (i, 0))
    full = lambda shp: pl.BlockSpec(shp, lambda i: (0,) * len(shp))
    outs = [((n, 1024), BF16), ((n, 512), F32), ((n, 512), F32), ((n, 512), BF16), ((n, 512), BF16),
            ((n, MLA_KVL), F32), ((n, MLA_ROPE), F32), ((n, MLA_HEADS * MLA_QW), BF16), ((n, MLA_QW), BF16)]
    return pl.pallas_call(
        _ab_post_body,
        grid=(n // t,),
        in_specs=[row(3072), row(128), row(128), full((1, MLA_QL)), full((1, MLA_KVL)),
                  full((MLA_QL, 3072)), full((MLA_HEADS, MLA_NOPE, MLA_KVL))],
        out_specs=[row(s[1]) for s, _ in outs],
        out_shape=[SDS(s, dt) for s, dt in outs],
        scratch_shapes=[pltpu.VMEM((MLA_QL, 3072), BF16), pltpu.VMEM((MLA_HEADS, MLA_NOPE, MLA_KVL), BF16)],
        compiler_params=_cp(("arbitrary",), 48),
        name="ab_post",
    )(proj, cos_t, sin_t, g_q_a.reshape(1, -1), g_kv_a.reshape(1, -1), wq_aug, wk_abs)


def _sb_block(z, strict, u_bf, c):
    sp = _softplus(z)
    lk = -sp if strict is None else jnp.where(strict, -sp, 0.0)
    hi, lo = _split_bf16(lk)
    after = _dot(hi, u_bf) + _dot(lo, u_bf)
    w = jnp.exp(z - sp + after + c)
    if strict is not None:
        w = jnp.where(strict, w, 0.0)
    return w, c + jnp.sum(lk, axis=1, keepdims=True)


def _upper_mask(n):
    r = lax.broadcasted_iota(I32, (n, n), 0)
    c = lax.broadcasted_iota(I32, (n, n), 1)
    return jnp.where(r > c, 1.0, 0.0).astype(BF16)


SBP_TQ, SBP_TK = 256, 256


def _sb_prompt_body(q_ref, k_ref, v_ref, o_ref, acc_ref, c_ref):
    tq, tk = SBP_TQ, SBP_TK
    qi = pl.program_id(2)
    q = q_ref[...]
    q2 = jnp.concatenate([q[:, :SB_DIM], q[:, SB_DIM:]], axis=0)
    acc_ref[...] = jnp.zeros_like(acc_ref)
    c_ref[...] = jnp.zeros_like(c_ref)
    qpos = qi * tq + (lax.broadcasted_iota(I32, (2 * tq, tk), 0) & (tq - 1))
    col = lax.broadcasted_iota(I32, (2 * tq, tk), 1)
    u_bf = _upper_mask(tk)
    nkb = (qi * tq + tq - 1) // tk + 1
    nfull = (qi * tq) // tk

    def block(kb, masked):
        ks = pl.multiple_of(kb * tk, tk)
        k = k_ref[pl.ds(ks, tk), :]
        v = v_ref[pl.ds(ks, tk), :]
        z = _dot_nt(q2, k) * SB_SCALE
        w, c_new = _sb_block(z, ((ks + col) < qpos) if masked else None, u_bf, c_ref[...])
        acc_ref[...] += _dot(_bf(w), v)
        c_ref[...] = c_new

    def masked_step(it, carry):
        block(nkb - 1 - it, True)
        return carry

    def full_step(it, carry):
        block(nfull - 1 - it, False)
        return carry

    lax.fori_loop(0, nkb - nfull, masked_step, 0)
    lax.fori_loop(0, nfull, full_step, 0)
    acc = acc_ref[...]
    o_ref[...] = _bf(jnp.concatenate([acc[:tq], acc[tq:]], axis=1))


def sb_prompt_call(tok, sbq, sbk_b, sbv_b):
    tq = SBP_TQ
    nq = tok.seq // tq
    return pl.pallas_call(
        _sb_prompt_body,
        grid=(tok.batch, SB_KV, nq),
        in_specs=[pl.BlockSpec((tq, SB_GROUP * SB_DIM), lambda b, h, i: (b * nq + i, h)),
                  pl.BlockSpec((tok.seq, SB_DIM), lambda b, h, i: (b, h)),
                  pl.BlockSpec((tok.seq, SB_DIM), lambda b, h, i: (b, h))],
        out_specs=pl.BlockSpec((tq, SB_GROUP * SB_DIM), lambda b, h, i: (b * nq + i, h)),
        out_shape=SDS((tok.n_p, SB_HEADS * SB_DIM), BF16),
        scratch_shapes=[pltpu.VMEM((2 * tq, SB_DIM), F32), pltpu.VMEM((2 * tq, 1), F32)],
        compiler_params=_cp(("arbitrary", "arbitrary", "arbitrary"), 40),
        name="sb_prompt",
    )(sbq, sbk_b, sbv_b)


MLAP_TQ, MLAP_TK = 128, 512


def _mla_prompt_body(q_ref, km_ref, wv_ref, o_ref, m_ref, l_ref, acc_ref):
    tq, tk = MLAP_TQ, MLAP_TK
    nh = MLA_HEADS
    qi = pl.program_id(1)
    q = q_ref[...]
    qs = jnp.concatenate([q[:, h * MLA_QW:(h + 1) * MLA_QW] for h in range(nh)], axis=0)
    m_ref[...] = jnp.full_like(m_ref, NEG)
    l_ref[...] = jnp.zeros_like(l_ref)
    acc_ref[...] = jnp.zeros_like(acc_ref)
    qpos = qi * tq + (lax.broadcasted_iota(I32, (nh * tq, tk), 0) & (tq - 1))
    col = lax.broadcasted_iota(I32, (nh * tq, tk), 1)
    nkb = (qi * tq + tq - 1) // tk + 1
    nfull = (qi * tq + 1) // tk

    def block(kb, masked):
        ks = pl.multiple_of(kb * tk, tk)
        kblk = km_ref[pl.ds(ks, tk), :]
        s = _dot_nt(qs, kblk) * MLA_SCALE
        if masked:
            s = jnp.where((ks + col) <= qpos, s, NEG)
        m_old = m_ref[...]
        m_new = jnp.maximum(m_old, jnp.max(s, axis=1, keepdims=True))
        alpha = jnp.exp(m_old - m_new)
        p = jnp.exp(s - m_new)
        l_ref[...] = alpha * l_ref[...] + jnp.sum(p, axis=1, keepdims=True)
        acc_ref[...] = alpha * acc_ref[...] + _dot(_bf(p), kblk[:, :MLA_KVL])
        m_ref[...] = m_new

    def full_step(kb, carry):
        block(kb, False)
        return carry

    def masked_step(kb, carry):
        block(kb, True)
        return carry

    lax.fori_loop(0, nfull, full_step, 0)
    lax.fori_loop(nfull, nkb, masked_step, 0)
    o_lat = acc_ref[...] / l_ref[...]
    for h in range(nh):
        o_ref[:, h * MLA_V:(h + 1) * MLA_V] = _bf(_dot(_bf(o_lat[h * tq:(h + 1) * tq]), _bf(wv_ref[h])))


def mla_prompt_call(tok, qm, km, wv):
    tq = MLAP_TQ
    nq = tok.seq // tq
    return pl.pallas_call(
        _mla_prompt_body,
        grid=(tok.batch, nq),
        in_specs=[pl.BlockSpec((tq, MLA_HEADS * MLA_QW), lambda b, i: (b * nq + i, 0)),
                  pl.BlockSpec((tok.seq, MLA_QW), lambda b, i: (b, 0)),
                  pl.BlockSpec((MLA_HEADS, MLA_KVL, MLA_V), lambda b, i: (0, 0, 0))],
        out_specs=pl.BlockSpec((tq, MLA_HEADS * MLA_V), lambda b, i: (b * nq + i, 0)),
        out_shape=SDS((tok.n_p, MLA_HEADS * MLA_V), BF16),
        scratch_shapes=[pltpu.VMEM((MLA_HEADS * tq, 1), F32), pltpu.VMEM((MLA_HEADS * tq, 1), F32),
                        pltpu.VMEM((MLA_HEADS * tq, MLA_KVL), F32)],
        compiler_params=_cp(("arbitrary", "arbitrary"), 40),
        name="mla_prompt",
    )(qm, km, wv)


DEC_PP = 8
DEC_SB_KEYS = 256
DEC_QROWS = 8


def _decode_pages_per_step(n_pages):
    pp = DEC_PP
    while pp > 1 and n_pages % (2 * pp) != 0:
        pp //= 2
    assert n_pages % (2 * pp) == 0
    return pp


def _decode_body(pt_ref, qsb_ref, qml_ref, nk_ref, nv_ref, nl_ref, nr_ref, wv_ref,
                 ck_hbm, cv_hbm, cl_hbm, cr_hbm, osb_ref, oml_ref,
                 kbuf, vbuf, lbuf, rbuf, sem, *, n_pages, layer):
    pp = _decode_pages_per_step(n_pages)
    nk = pp * PAGE
    b = pl.program_id(0)
    nb = pl.num_programs(0)
    steps = n_pages // pp
    srows = SB_KV * DEC_QROWS
    mrows = MLA_HEADS * DEC_QROWS

    def copies(bb, s, slot):
        out = []
        for j in range(pp):
            page = pt_ref[bb * n_pages + s * pp + j]
            rows = pl.ds(j * PAGE, PAGE)
            out.append(pltpu.make_async_copy(ck_hbm.at[layer, page], kbuf.at[slot, rows], sem.at[0, slot]))
            out.append(pltpu.make_async_copy(cv_hbm.at[layer, page], vbuf.at[slot, rows], sem.at[1, slot]))
            out.append(pltpu.make_async_copy(cl_hbm.at[layer, page], lbuf.at[slot, rows], sem.at[2, slot]))
            out.append(pltpu.make_async_copy(cr_hbm.at[layer, page], rbuf.at[slot, :, rows], sem.at[3, slot]))
        return out

    @pl.when(b == 0)
    def _():
        for cp in copies(0, steps - 1, (steps - 1) % 2):
            cp.start()

    qsb = qsb_ref[...]
    qml = qml_ref[...]
    q_lat = qml[:, :MLA_KVL]
    q_rope = qml[:, MLA_KVL:MLA_KVL + MLA_ROPE]
    sbk = DEC_SB_KEYS
    nsb = nk * SB_KV // sbk
    u_bf = _upper_mask(sbk)

    def own_head(nrows):
        r = lax.broadcasted_iota(I32, (nrows, sbk), 0)
        cidx = lax.broadcasted_iota(I32, (nrows, sbk), 1)
        return ((r & (srows - 1)) // DEC_QROWS) == (cidx & (SB_KV - 1))

    tq_sb = lax.broadcasted_iota(I32, (srows, sbk), 0) & (DEC_QROWS // SB_GROUP - 1)
    tok_sb = lax.broadcasted_iota(I32, (srows, sbk), 1) // SB_KV
    tq_ml = lax.broadcasted_iota(I32, (mrows, PAGE), 0) & (DEC_QROWS - 1)
    col_ml = lax.broadcasted_iota(I32, (mrows, PAGE), 1)
    z = _dot_nt(qsb, _bf(nk_ref[0:sbk, :])) * SB_SCALE
    w, c0 = _sb_block(z, jnp.logical_and(own_head(srows), tok_sb < tq_sb), u_bf, jnp.zeros((srows, 1), F32))
    acc0 = _dot(_bf(w), _bf(nv_ref[0:sbk, :]))

    nl = _bf(nl_ref[...])
    s = (_dot_nt(q_lat, nl) + _dot(q_rope, _bf(nr_ref[...]))) * MLA_SCALE
    s = jnp.where(col_ml <= tq_ml, s, NEG)
    m0 = jnp.max(s, axis=1, keepdims=True)
    p = jnp.exp(s - m0)
    l0 = jnp.sum(p, axis=1, keepdims=True)
    macc0 = _dot(_bf(p), nl)

    own = own_head(nsb * srows)

    def step(it, carry):
        c, acc, m, l, macc = carry
        s_idx = steps - 1 - it
        slot = s_idx % 2
        for cp in copies(b, s_idx, slot):
            cp.wait()
        more = s_idx > 0
        for cp in copies(jnp.where(more, b, jnp.minimum(b + 1, nb - 1)), jnp.where(more, s_idx - 1, steps - 1), 1 - slot):
            cp.start()

        z = _dot_nt(qsb, _bf(kbuf[slot].reshape(nk * SB_KV, SB_DIM))) * SB_SCALE
        sp = _softplus(z)
        stack = lambda a: jnp.concatenate([a[:, i * sbk:(i + 1) * sbk] for i in range(nsb)], axis=0)
        lk = jnp.where(own, stack(-sp), 0.0)
        hi, lo = _split_bf16(lk)
        after = _dot(hi, u_bf) + _dot(lo, u_bf)
        tot = jnp.sum(lk, axis=1, keepdims=True)
        carries = [None] * nsb
        for i in reversed(range(nsb)):
            carries[i] = c
            c = c + tot[i * srows:(i + 1) * srows]
        ws = jnp.where(own, jnp.exp(stack(z - sp) + after + jnp.concatenate(carries, axis=0)), 0.0)
        w = jnp.concatenate([ws[i * srows:(i + 1) * srows] for i in range(nsb)], axis=1)
        acc = acc + _dot(_bf(w), _bf(vbuf[slot].reshape(nk * SB_KV, SB_DIM)))

        lat = _bf(lbuf[slot])
        s = (_dot_nt(q_lat, lat) + _dot(q_rope, _bf(rbuf[slot]))) * MLA_SCALE
        m_new = jnp.maximum(m, jnp.max(s, axis=1, keepdims=True))
        alpha = jnp.exp(m - m_new)
        p = jnp.exp(s - m_new)
        l = alpha * l + jnp.sum(p, axis=1, keepdims=True)
        macc = alpha * macc + _dot(_bf(p), lat)
        return c, acc, m_new, l, macc

    c, acc, m, l, macc = lax.fori_loop(0, steps, step, (c0, acc0, m0, l0, macc0))

    @pl.when(b == nb - 1)
    def _():
        for cp in copies(b, steps - 1, (steps - 1) % 2):
            cp.wait()

    for h in range(SB_KV):
        osb_ref[h] = _bf(acc[h * DEC_QROWS:(h + 1) * DEC_QROWS])
    o_lat = _bf(macc / l)
    for h in range(MLA_HEADS):
        oml_ref[:, h * MLA_V:(h + 1) * MLA_V] = _bf(_dot(o_lat[h * DEC_QROWS:(h + 1) * DEC_QROWS], _bf(wv_ref[h])))


def decode_call(page_table, qsb_d, qml_d, nk, nv, nl, nr, wv, cache_k, cache_v, cache_l, cache_r, layer):
    db, n_pages = page_table.shape
    pp = _decode_pages_per_step(n_pages)
    blk = lambda shp: pl.BlockSpec((None,) + shp, lambda b, pt: (b,) + (0,) * len(shp))
    grid_spec = pltpu.PrefetchScalarGridSpec(
        num_scalar_prefetch=1,
        grid=(db,),
        in_specs=[blk((SB_KV * DEC_QROWS, SB_DIM)), blk((MLA_HEADS * DEC_QROWS, MLA_QW)),
                  blk((PAGE * SB_KV, SB_DIM)), blk((PAGE * SB_KV, SB_DIM)), blk((PAGE, MLA_KVL)), blk((MLA_ROPE, PAGE)),
                  pl.BlockSpec((MLA_HEADS, MLA_KVL, MLA_V), lambda b, pt: (0, 0, 0)),
                  pl.BlockSpec(memory_space=pl.ANY), pl.BlockSpec(memory_space=pl.ANY),
                  pl.BlockSpec(memory_space=pl.ANY), pl.BlockSpec(memory_space=pl.ANY)],
        out_specs=[blk((SB_KV, DEC_QROWS, SB_DIM)), blk((DEC_QROWS, MLA_HEADS * MLA_V))],
        scratch_shapes=[pltpu.VMEM((2, pp * PAGE, SB_KV, SB_DIM), F32), pltpu.VMEM((2, pp * PAGE, SB_KV, SB_DIM), F32),
                        pltpu.VMEM((2, pp * PAGE, MLA_KVL), F32), pltpu.VMEM((2, MLA_ROPE, pp * PAGE), F32),
                        pltpu.SemaphoreType.DMA((4, 2))],
    )
    return pl.pallas_call(
        functools.partial(_decode_body, n_pages=n_pages, layer=layer),
        grid_spec=grid_spec,
        out_shape=[SDS((db, SB_KV, DEC_QROWS, SB_DIM), BF16), SDS((db, DEC_QROWS, MLA_HEADS * MLA_V), BF16)],
        compiler_params=_cp(("arbitrary",), 40),
        name="decode_attn",
    )(page_table.reshape(-1), qsb_d, qml_d, nk, nv, nl, nr, wv, cache_k, cache_v, cache_l, cache_r)


def _scan_chunk(q, k, v, g, s, valid_rows=None):
    c, dk = q.shape
    sub = min(SCAN_SUB, c)
    nsub = c // sub
    row = lax.broadcasted_iota(I32, (c, c), 0)
    col = lax.broadcasted_iota(I32, (c, c), 1)
    if valid_rows is not None:
        live = lax.broadcasted_iota(I32, (c, dk), 0) < valid_rows
        g = jnp.where(live, g, 0.0)
        k = jnp.where(live, k, 0.0)
    tri = jnp.where(row >= col, 1.0, 0.0).astype(BF16)
    g_hi, g_lo = _split_bf16(g)
    bcum = _dot(tri, g_hi) + _dot(tri, g_lo)
    b_last = bcum[c - 1:c, :]
    rowk = lax.broadcasted_iota(I32, (c, dk), 0)
    lhs, rhs = [], []
    for j in range(nsub):
        b0 = bcum[j * sub:j * sub + 1, :]
        lhs.append(_bf(q * jnp.exp(jnp.minimum(bcum - b0, 0.0))))
        in_j = (rowk >= j * sub) & (rowk < (j + 1) * sub)
        rhs.append(_bf(jnp.where(in_j, k * jnp.exp(jnp.where(in_j, b0 - bcum, 0.0)), 0.0)))
    att = _dot_nt(jnp.concatenate(lhs, axis=1), jnp.concatenate(rhs, axis=1))
    att = jnp.where(row >= col, att, 0.0)
    v_bf = _bf(v)
    o = _dot(_bf(att), v_bf) + _dot(_bf(q * jnp.exp(bcum)), _bf(s))
    kd = _bf(k * jnp.exp(b_last - bcum))
    decay_col = jnp.transpose(jnp.broadcast_to(jnp.exp(b_last), (8, dk)))[:, 0:1]
    s_new = decay_col * s + _dot_tn(kd, v_bf)
    return o, s_new


def _gla_inputs(q_ref, k_ref, ga_ref, wg_ref, bg_ref):
    q = q_ref[...] * (GLA_DK ** -0.5)
    k = k_ref[...]
    x = _dot(_bf(ga_ref[...]), _bf(wg_ref[...])) + bg_ref[...]
    g = -_softplus(-x) / GLA_NORM
    return q, k, g


def _hgrn_inputs(q_ref, f_ref, lb_ref):
    q = _silu(q_ref[...]) * (HGRN_DK ** -0.5)
    lb = lb_ref[...]
    f = lb + (1.0 - lb) * _sigmoid(f_ref[...])
    return q, 1.0 - f, jnp.log(f)


def _scan_out(o, gain_ref, r_ref):
    return _bf(_rms(o, gain_ref[...]) * _silu(r_ref[...]))


SCAN_TB = 512


def _scan_prompt_body(*refs, kind):
    if kind == "gla":
        q_ref, k_ref, v_ref, r_ref, ga_ref, wg_ref, bg_ref, gain_ref, o_ref, st_ref, s_ref, obuf = refs
        q, k, g = _gla_inputs(q_ref, k_ref, ga_ref, wg_ref, bg_ref)
    else:
        q_ref, f_ref, v_ref, r_ref, lb_ref, gain_ref, o_ref, st_ref, s_ref, obuf = refs
        q, k, g = _hgrn_inputs(q_ref, f_ref, lb_ref)
    v = v_ref[...]
    tb = pl.program_id(2)

    @pl.when(tb == 0)
    def _():
        s_ref[...] = jnp.zeros_like(s_ref)

    cc = SCAN_CHUNK
    for ci in range(SCAN_TB // cc):
        sl = slice(ci * cc, (ci + 1) * cc)
        o, s_new = _scan_chunk(q[sl], k[sl], v[sl], g[sl], s_ref[...])
        s_ref[...] = s_new
        obuf[sl, :] = o
    o_ref[...] = _scan_out(obuf[...], gain_ref, r_ref)

    @pl.when(tb == pl.num_programs(2) - 1)
    def _():
        st_ref[...] = s_ref[...]


def scan_prompt_call(tok, proj, ga_proj, w_gate_pad, b_gate, gain, lb, kind):
    tb = SCAN_TB
    ntb = tok.seq // tb
    rows = lambda w, c0: pl.BlockSpec((tb, w), lambda b, h, t: (b * ntb + t, c0 // w + h))
    if kind == "gla":
        nh, dk, dv = GLA_HEADS, GLA_DK, GLA_DV
        in_specs = [rows(dk, 0), rows(dk, 512), rows(dv, 1024), rows(dv, 2048),
                    pl.BlockSpec((tb, 128), lambda b, h, t: (b * ntb + t, 0)),
                    pl.BlockSpec((128, dk), lambda b, h, t: (0, h)), pl.BlockSpec((1, dk), lambda b, h, t: (0, h)),
                    pl.BlockSpec((1, dv), lambda b, h, t: (0, 0))]
        args = (proj, proj, proj, proj, ga_proj, w_gate_pad, b_gate.reshape(1, -1), gain.reshape(1, -1))
    else:
        nh, dk, dv = HGRN_HEADS, HGRN_DK, HGRN_DV
        in_specs = [rows(dk, 3072), rows(dk, 4096), rows(dv, 5120), rows(dv, 6144),
                    pl.BlockSpec((1, dk), lambda b, h, t: (0, h)), pl.BlockSpec((1, dv), lambda b, h, t: (0, 0))]
        args = (proj, proj, proj, proj, lb.reshape(1, -1), gain.reshape(1, -1))
    return pl.pallas_call(
        functools.partial(_scan_prompt_body, kind=kind),
        grid=(tok.batch, nh, ntb),
        in_specs=in_specs,
        out_specs=[pl.BlockSpec((tb, dv), lambda b, h, t: (b * ntb + t, h)),
                   pl.BlockSpec((None, None, dk, dv), lambda b, h, t: (b, h, 0, 0))],
        out_shape=[SDS((tok.n_p, nh * dv), BF16), SDS((tok.batch, nh, dk, dv), F32)],
        scratch_shapes=[pltpu.VMEM((dk, dv), F32), pltpu.VMEM((tb, dv), F32)],
        compiler_params=_cp(("arbitrary", "arbitrary", "arbitrary"), 40),
        name="scan_prompt_" + kind,
    )(*args)


SCAN_SROWS = 8


def _scan_sample_body(*refs, kind, valid):
    if kind == "gla":
        p_ref, ga_ref, wg_ref, bg_ref, gain_ref, s0_ref, o_ref, st_ref = refs
        nh, dk, dv = GLA_HEADS, GLA_DK, GLA_DV
    else:
        p_ref, lb_ref, gain_ref, s0_ref, o_ref, st_ref = refs
        nh, dk, dv = HGRN_HEADS, HGRN_DK, HGRN_DV
    for h in range(nh):
        if kind == "gla":
            q = p_ref[:, h * dk:(h + 1) * dk] * (GLA_DK ** -0.5)
            k = p_ref[:, 512 + h * dk:512 + (h + 1) * dk]
            v = p_ref[:, 1024 + h * dv:1024 + (h + 1) * dv]
            r = p_ref[:, 2048 + h * dv:2048 + (h + 1) * dv]
            x = _dot(_bf(ga_ref[...]), _bf(wg_ref[:, h * dk:(h + 1) * dk])) + bg_ref[:, h * dk:(h + 1) * dk]
            g = -_softplus(-x) / GLA_NORM
        else:
            q = _silu(p_ref[:, 3072 + h * dk:3072 + (h + 1) * dk]) * (HGRN_DK ** -0.5)
            lb = lb_ref[:, h * dk:(h + 1) * dk]
            f = lb + (1.0 - lb) * _sigmoid(p_ref[:, 4096 + h * dk:4096 + (h + 1) * dk])
            k = 1.0 - f
            g = jnp.log(f)
            v = p_ref[:, 5120 + h * dv:5120 + (h + 1) * dv]
            r = p_ref[:, 6144 + h * dv:6144 + (h + 1) * dv]
        o, s_new = _scan_chunk(q, k, v, g, s0_ref[h], valid_rows=valid)
        st_ref[h] = s_new
        o_ref[:, h * dv:(h + 1) * dv] = _bf(_rms(o, gain_ref[...]) * _silu(r))


def scan_sample_call(proj_s, ga_s, w_gate_pad, b_gate, gain, lb, s0, kind, valid):
    db = proj_s.shape[0]
    r = SCAN_SROWS
    full = lambda shp: pl.BlockSpec(shp, lambda b: (0,) * len(shp))
    per_b = lambda shp: pl.BlockSpec((None,) + shp, lambda b: (b,) + (0,) * len(shp))
    if kind == "gla":
        nh, dk, dv = GLA_HEADS, GLA_DK, GLA_DV
        in_specs = [per_b((r, proj_s.shape[2])), per_b((r, 128)), full((128, nh * dk)), full((1, nh * dk)), full((1, dv)),
                    per_b((nh, dk, dv))]
        args = (proj_s, ga_s, w_gate_pad, b_gate.reshape(1, -1), gain.reshape(1, -1), s0)
    else:
        nh, dk, dv = HGRN_HEADS, HGRN_DK, HGRN_DV
        in_specs = [per_b((r, proj_s.shape[2])), full((1, nh * dk)), full((1, dv)), per_b((nh, dk, dv))]
        args = (proj_s, lb.reshape(1, -1), gain.reshape(1, -1), s0)
    return pl.pallas_call(
        functools.partial(_scan_sample_body, kind=kind, valid=valid),
        grid=(db,),
        in_specs=in_specs,
        out_specs=[per_b((r, nh * dv)), per_b((nh, dk, dv))],
        out_shape=[SDS((db, r, nh * dv), BF16), SDS((db, nh, dk, dv), F32)],
        compiler_params=_cp(("arbitrary",), 40),
        name="scan_sample_" + kind,
    )(*args)


def _moe_prep_body(x_ref, g_ref, scp_ref, scs_ref, shp_ref, shs_ref, wr_ref, br_ref,
                   hp_ref, sel_ref, idx_ref, wt_ref, *, p_tiles):
    is_s = pl.program_id(0) >= p_tiles
    y = _rms(x_ref[...], g_ref[...])
    sc = jnp.where(is_s, scs_ref[...], scp_ref[...])
    sh = jnp.where(is_s, shs_ref[...], shp_ref[...])
    h = y * (1.0 + sc) + sh
    h_hi, h_lo = _split_bf16(h)
    half = D_MODEL // 2
    hr = h_hi.astype(F32)
    lo_bits = lax.shift_right_logical(lax.bitcast_convert_type(hr[:, :half], U32), jnp.uint32(16))
    hi_bits = lax.bitcast_convert_type(hr[:, half:], U32)
    hp_ref[...] = hi_bits | lo_bits
    w_hi, w_lo = _split_bf16(wr_ref[...])
    logits = _dot(h_hi, w_hi) + _dot(h_lo, w_hi) + _dot(h_hi, w_lo) + br_ref[...]
    t, ne = logits.shape
    lane = lax.broadcasted_iota(I32, (t, ne), 1)
    lane4 = lax.broadcasted_iota(I32, (t, TOP_K), 1)
    vals = logits
    sel = jnp.zeros((t, ne), F32)
    idx4 = jnp.zeros((t, TOP_K), I32)
    e4 = jnp.zeros((t, TOP_K), F32)
    m0 = None
    for kk in range(TOP_K):
        m = jnp.max(vals, axis=1, keepdims=True)
        ik = jnp.min(jnp.where(vals == m, lane, ne), axis=1, keepdims=True)
        hot = lane == ik
        vals = jnp.where(hot, -jnp.inf, vals)
        sel = jnp.where(hot, 1.0, sel)
        if kk == 0:
            m0 = m
        idx4 = jnp.where(lane4 == kk, ik, idx4)
        e4 = jnp.where(lane4 == kk, jnp.exp(m - m0), e4)
    sel_ref[...] = sel.astype(sel_ref.dtype)
    idx_ref[...] = idx4
    wt_ref[...] = e4 / jnp.sum(e4, axis=1, keepdims=True)


def moe_prep_call(tok, x, g, mod_p, mod_s, w_router, b_router):
    d = D_MODEL
    row = lambda w: pl.BlockSpec((TM, w), lambda i: (i, 0))
    return pl.pallas_call(
        functools.partial(_moe_prep_body, p_tiles=tok.p_tiles),
        grid=(tok.tiles,),
        in_specs=[row(d), pl.BlockSpec((1, d), lambda i: (0, 0))] + tok.mod_specs(4, d) + tok.mod_specs(3, d)
        + [pl.BlockSpec((d, N_EXPERTS), lambda i: (0, 0)), pl.BlockSpec((1, N_EXPERTS), lambda i: (0, 0))],
        out_specs=[row(d // 2), row(N_EXPERTS), row(TOP_K), row(TOP_K)],
        out_shape=[SDS((tok.n, d // 2), U32), SDS((tok.n, N_EXPERTS), BF16), SDS((tok.n, TOP_K), I32),
                   SDS((tok.n, TOP_K), F32)],
        compiler_params=_cp(("arbitrary",), 48),
        name="moe_prep",
    )(x, g.reshape(1, d), mod_p, mod_s, mod_p, mod_s, w_router, b_router.reshape(1, -1))


RANK_TM = 256


def _moe_rank_body(sel_ref, rank_ref, cnt_ref):
    @pl.when(pl.program_id(0) == 0)
    def _():
        cnt_ref[...] = jnp.zeros_like(cnt_ref)

    sel = sel_ref[...]
    t = sel.shape[0]
    r = lax.broadcasted_iota(I32, (t, t), 0)
    c = lax.broadcasted_iota(I32, (t, t), 1)
    before = jnp.where(r > c, 1.0, 0.0).astype(BF16)
    base = cnt_ref[...]
    rank_ref[...] = base + _dot(before, sel)
    cnt_ref[...] = base + jnp.sum(sel.astype(F32), axis=0, keepdims=True)


def moe_rank_call(sel):
    n, ne = sel.shape
    t = RANK_TM
    return pl.pallas_call(
        _moe_rank_body,
        grid=(n // t,),
        in_specs=[pl.BlockSpec((t, ne), lambda i: (i, 0))],
        out_specs=[pl.BlockSpec((t, ne), lambda i: (i, 0)), pl.BlockSpec((1, ne), lambda i: (0, 0))],
        out_shape=[SDS((n, ne), F32), SDS((1, ne), F32)],
        compiler_params=_cp(("arbitrary",), 32),
        name="moe_rank",
    )(sel)


DISP_TM = 256


def _zero_fill(used, tail_ref, dst_hbm, zrows, sem):
    n_rows = dst_hbm.shape[0]
    zrows[...] = jnp.zeros_like(zrows)

    def zcopy(row):
        return pltpu.make_async_copy(zrows, dst_hbm.at[pl.ds(pl.multiple_of(row, ROW_ALIGN), ROW_ALIGN)], sem)

    for act in ("start", "wait"):
        if tail_ref is not None:
            for e in range(N_EXPERTS):
                @pl.when(tail_ref[e] >= 0)
                def _(e=e, act=act):
                    getattr(zcopy(tail_ref[e]), act)()

        for r in range(N_EXPERTS):
            @pl.when(used + r * ROW_ALIGN < n_rows)
            def _(r=r, act=act):
                getattr(zcopy(used + r * ROW_ALIGN), act)()


def _moe_dispatch_body(tail_ref, pos_ref, h_ref, xs_hbm, zrows, sem):
    i = pl.program_id(0)
    t = DISP_TM

    @pl.when(i == 0)
    def _():
        _zero_fill(tail_ref[N_EXPERTS], tail_ref, xs_hbm, zrows, sem)

    def copy(tt, kk):
        return pltpu.make_async_copy(h_ref.at[pl.ds(tt, 1)], xs_hbm.at[pl.ds(pos_ref[tt * TOP_K + kk], 1)], sem)

    def issue(tt, c):
        for kk in range(TOP_K):
            copy(tt, kk).start(priority=kk % 2)
        return c

    def drain(tt, c):
        for kk in range(TOP_K):
            copy(tt, kk).wait()
        return c

    lax.fori_loop(0, t, issue, 0)
    lax.fori_loop(0, t, drain, 0)


def moe_dispatch_call(tail_rows, pos_flat, hp, n_rows):
    n, w = hp.shape
    t = DISP_TM
    grid_spec = pltpu.PrefetchScalarGridSpec(
        num_scalar_prefetch=1,
        grid=(n // t,),
        in_specs=[pl.BlockSpec((t * TOP_K,), lambda i, tl: (i,), memory_space=pltpu.SMEM),
                  pl.BlockSpec((t, w), lambda i, tl: (i, 0))],
        out_specs=pl.BlockSpec(memory_space=pl.ANY),
        scratch_shapes=[pltpu.VMEM((ROW_ALIGN, w), U32), pltpu.SemaphoreType.DMA(())],
    )
    return pl.pallas_call(
        _moe_dispatch_body,
        grid_spec=grid_spec,
        out_shape=SDS((n_rows, w), U32),
        compiler_params=_cp(("arbitrary",), 32),
        name="moe_dispatch",
    )(tail_rows, pos_flat, hp)


def _row_copies(src, src_off, dst, dst_off, nrows, sem, chunk, act):
    nfull = nrows // chunk

    def body(j, c):
        o = j * chunk
        act(pltpu.make_async_copy(src.at[pl.ds(pl.multiple_of(src_off + o, ROW_ALIGN), chunk)],
                                  dst.at[pl.ds(pl.multiple_of(dst_off + o, ROW_ALIGN), chunk)], sem))
        return c

    lax.fori_loop(0, nfull, body, 0)
    rem = nrows - nfull * chunk
    done = nfull * chunk
    bit = chunk // 2
    while bit >= ROW_ALIGN:
        take = (rem & bit) != 0

        @pl.when(take)
        def _(done=done, bit=bit):
            act(pltpu.make_async_copy(src.at[pl.ds(pl.multiple_of(src_off + done, ROW_ALIGN), bit)],
                                      dst.at[pl.ds(pl.multiple_of(dst_off + done, ROW_ALIGN), bit)], sem))

        done = done + jnp.where(take, bit, 0)
        bit //= 2


def _moe_expert_body(ie_ref, if_ref, start_ref, rows_ref, used_ref,
                     xs_hbm, wg_ref, wl_ref, bg_ref, bl_ref, wd_ref, bd_ref, ys_hbm,
                     xin, yacc, zrows, sem):
    it = pl.program_id(0)
    f = pl.program_id(1)
    nf = pl.num_programs(1)
    rows = rows_ref[it]
    start = start_ref[it]
    half = D_MODEL // 2
    sub = MOE_SUB
    hsub = sub // 2
    n_full = rows // sub
    rem = rows - n_full * sub
    n_big = n_full + jnp.where(rem > hsub, 1, 0)
    has_small = jnp.logical_and(rem > 0, rem <= hsub)

    @pl.when(jnp.logical_and(it == 0, f == 0))
    def _():
        xin[...] = jnp.zeros_like(xin)

    def wait_out(item):
        _row_copies(yacc, 0, ys_hbm, start_ref[item], rows_ref[item], sem.at[1], sub, lambda cp: cp.wait())

    prev = jnp.maximum(it - 1, 0)

    @pl.when(jnp.logical_and(f == 0, jnp.logical_and(it > 0, rows_ref[prev] > 0)))
    def _():
        wait_out(prev)

    @pl.when(rows > 0)
    def _():
        @pl.when(f == 0)
        def _():
            _row_copies(xs_hbm, start, xin, 0, rows, sem.at[0], sub, lambda cp: cp.start())

            def init(r, c):
                yacc[pl.ds(pl.multiple_of(r * hsub, hsub), hsub), :] = jnp.broadcast_to(bd_ref[...], (hsub, D_MODEL))
                return c

            lax.fori_loop(0, (rows + hsub - 1) // hsub, init, 0)
            _row_copies(xs_hbm, start, xin, 0, rows, sem.at[0], sub, lambda cp: cp.wait())

        def tile(o, n):
            w = xin[pl.ds(o, n), :]
            x_lo = lax.bitcast_convert_type(lax.shift_left(w, jnp.uint32(16)), F32)
            x_hi = lax.bitcast_convert_type(w & jnp.uint32(0xFFFF0000), F32)
            gate = _dot(x_lo, wg_ref[0:half, :]) + _dot(x_hi, wg_ref[half:, :]) + bg_ref[...]
            lin = _dot(x_lo, wl_ref[0:half, :]) + _dot(x_hi, wl_ref[half:, :]) + bl_ref[...]
            gate = jnp.minimum(gate, SWIGLU_LIMIT)
            lin = jnp.clip(lin, -SWIGLU_LIMIT, SWIGLU_LIMIT)
            act = (lin + 1.0) * gate * _sigmoid(SWIGLU_ALPHA * gate)
            yacc[pl.ds(o, n), :] += _dot(act, wd_ref[...])

        def pair(p, c):
            o = pl.multiple_of(p * (2 * sub), 2 * sub)
            tile(o, sub)
            tile(o + sub, sub)
            return c

        lax.fori_loop(0, n_big // 2, pair, 0)

        @pl.when(n_big % 2 == 1)
        def _():
            tile(pl.multiple_of((n_big - 1) * sub, sub), sub)

        @pl.when(has_small)
        def _():
            tile(pl.multiple_of(n_full * sub, sub), hsub)

        @pl.when(f == nf - 1)
        def _():
            _row_copies(yacc, 0, ys_hbm, start, rows, sem.at[1], sub, lambda cp: cp.start())

    @pl.when(jnp.logical_and(it == pl.num_programs(0) - 1, f == nf - 1))
    def _():
        @pl.when(rows > 0)
        def _():
            wait_out(it)

        _zero_fill(used_ref[0], None, ys_hbm, zrows, sem.at[1])


def moe_expert_call(layer, item_e, item_f, item_start, item_rows, used, xs, w_gate_up, b_gate_up, w_down, b_down):
    nl, ne, d, ff2 = w_gate_up.shape
    ff = ff2 // 2
    nf = ff // MOE_F
    n_items = item_e.shape[0]
    n_rows = xs.shape[0]

    def wmap(col0, kind):
        def index_map(i, f, ie, ife, st, rw, us):
            fi = jnp.where(ife[i] >= 0, f, nf - 1)
            if kind == "col":
                return layer, ie[i], 0, col0 + fi
            if kind == "row":
                return layer, ie[i], fi, 0
            return layer, ie[i], 0, 0
        return index_map

    grid_spec = pltpu.PrefetchScalarGridSpec(
        num_scalar_prefetch=5,
        grid=(n_items, nf),
        in_specs=[pl.BlockSpec(memory_space=pl.ANY),
                  pl.BlockSpec((None, None, d, MOE_F), wmap(0, "col")),
                  pl.BlockSpec((None, None, d, MOE_F), wmap(nf, "col")),
                  pl.BlockSpec((None, None, 1, MOE_F), wmap(0, "col")),
                  pl.BlockSpec((None, None, 1, MOE_F), wmap(nf, "col")),
                  pl.BlockSpec((None, None, MOE_F, d), wmap(0, "row")),
                  pl.BlockSpec((None, None, 1, d), wmap(0, "fixed"))],
        out_specs=pl.BlockSpec(memory_space=pl.ANY),
        scratch_shapes=[pltpu.VMEM((MOE_ROWS, d // 2), U32), pltpu.VMEM((MOE_ROWS, d), F32),
                        pltpu.VMEM((ROW_ALIGN, d), F32), pltpu.SemaphoreType.DMA((2,))],
    )
    return pl.pallas_call(
        _moe_expert_body,
        grid_spec=grid_spec,
        out_shape=SDS((n_rows, d), F32),
        compiler_params=_cp(("arbitrary", "arbitrary"), MOE_VMEM_MIB),
        name="moe_experts",
    )(item_e, item_f, item_start, item_rows, used, xs, w_gate_up, w_gate_up,
      b_gate_up.reshape(nl, ne, 1, ff2), b_gate_up.reshape(nl, ne, 1, ff2), w_down, b_down.reshape(nl, ne, 1, d))


COMB_TM = 256


def _moe_combine_body(pos_ref, ys_hbm, x_ref, wt_ref, gp_ref, gs_ref, gf_ref, *rest, p_tiles, tpb, final):
    if final:
        yp_ref, ysm_ref, buf, sem = rest
    else:
        o_ref, buf, sem = rest
    i = pl.program_id(0)
    t = COMB_TM

    def copy(tt, kk):
        return pltpu.make_async_copy(ys_hbm.at[pl.ds(pos_ref[tt * TOP_K + kk], 1)], buf.at[kk, pl.ds(tt, 1)], sem)

    def issue(tt, c):
        for kk in range(TOP_K):
            copy(tt, kk).start(priority=kk % 2)
        return c

    def drain(tt, c):
        for kk in range(TOP_K):
            copy(tt, kk).wait()
        return c

    lax.fori_loop(0, t, issue, 0)
    lax.fori_loop(0, t, drain, 0)
    wt = wt_ref[...]
    moe = wt[:, 0:1] * buf[0]
    for kk in range(1, TOP_K):
        moe = moe + wt[:, kk:kk + 1] * buf[kk]
    is_s = i >= p_tiles * tpb
    gate = jnp.where(is_s, gs_ref[...], gp_ref[...])
    x = x_ref[...] + gate * moe
    if final:
        @pl.when(jnp.logical_not(is_s))
        def _():
            yp_ref[...] = _rms(x, gf_ref[...])

        @pl.when(is_s)
        def _():
            ysm_ref[...] = _rms(x, gf_ref[...])
    else:
        o_ref[...] = x


def moe_combine_call(tok, pos_flat, ys, x, wt, mod_p, mod_s, g_final, final):
    d = D_MODEL
    t = COMB_TM
    tpb = TM // t
    np_t = tok.p_tiles * tpb
    row = lambda w: pl.BlockSpec((t, w), lambda i: (i, 0))
    if final:
        out_specs = [pl.BlockSpec((t, d), lambda i: (jnp.minimum(i, np_t - 1), 0)),
                     pl.BlockSpec((t, d), lambda i: (jnp.maximum(i - np_t, 0), 0))]
        out_shape = [SDS((tok.n_p, d), F32), SDS((tok.n_s, d), F32)]
    else:
        out_specs = [row(d)]
        out_shape = [SDS((tok.n, d), F32)]

    def pmap(i):
        return jnp.minimum(i // (tok.tiles_per_batch * tpb), tok.batch - 1), 0, 5

    def smap(i):
        return jnp.maximum(i - tok.p_tiles * tpb, 0), 5

    return pl.pallas_call(
        functools.partial(_moe_combine_body, p_tiles=tok.p_tiles, tpb=tpb, final=final),
        grid=(tok.n // t,),
        in_specs=[pl.BlockSpec((t * TOP_K,), lambda i: (i,), memory_space=pltpu.SMEM),
                  pl.BlockSpec(memory_space=pl.ANY), row(d), row(TOP_K),
                  pl.BlockSpec((None, 1, d), pmap), pl.BlockSpec((t, d), smap),
                  pl.BlockSpec((1, d), lambda i: (0, 0))],
        out_specs=out_specs,
        out_shape=out_shape,
        scratch_shapes=[pltpu.VMEM((TOP_K, t, d), F32), pltpu.SemaphoreType.DMA(())],
        compiler_params=_cp(("arbitrary",), 48),
        name="moe_combine",
    )(pos_flat, ys, x, wt, mod_p, mod_s, g_final.reshape(1, d))


def moe_layer(tok, layer, x, g, mod_p, mod_s, w_router, b_router, w_gate_up, b_gate_up, w_down, b_down, g_final, final):
    hp, sel, idx4, wt = moe_prep_call(tok, x, g, mod_p, mod_s, w_router[layer], b_router[layer])
    rank, cnt = moe_rank_call(sel)
    real = cnt[0].astype(I32)
    counts = (real + ROW_ALIGN - 1) // ROW_ALIGN * ROW_ALIGN
    offs = jnp.cumsum(counts) - counts
    pos = jnp.take_along_axis(rank.astype(I32) + offs[None, :], idx4, axis=1)
    pos_flat = pos.reshape(-1)
    tail_rows = jnp.where(real > 0, offs + counts - ROW_ALIGN, -1).astype(I32)
    n_rows = tok.n * TOP_K + N_EXPERTS * ROW_ALIGN
    n_items = N_EXPERTS + n_rows // MOE_ROWS
    passes = (counts + MOE_ROWS - 1) // MOE_ROWS
    pend = jnp.cumsum(passes)
    total = pend[-1]
    ii = jnp.arange(n_items, dtype=I32)
    e_of = jnp.minimum(jnp.sum((pend[None, :] <= ii[:, None]).astype(I32), axis=1), N_EXPERTS - 1)
    j_of = ii - (pend - passes)[e_of]
    valid = ii < total
    last_e = e_of[jnp.maximum(total - 1, 0)]
    item_e = jnp.where(valid, e_of, last_e)
    item_f = jnp.where(valid, 0, -1).astype(I32)
    item_start = jnp.where(valid, offs[e_of] + j_of * MOE_ROWS, 0).astype(I32)
    item_rows = jnp.where(valid, jnp.minimum(counts[e_of] - j_of * MOE_ROWS, MOE_ROWS), 0).astype(I32)
    used = jnp.sum(counts).astype(I32).reshape(1)
    xs = moe_dispatch_call(jnp.concatenate([tail_rows, used]), pos_flat, hp, n_rows)
    ys = moe_expert_call(layer, item_e, item_f, item_start, item_rows, used, xs, w_gate_up, b_gate_up, w_down, b_down)
    return moe_combine_call(tok, pos_flat, ys, x, wt, mod_p, mod_s, g_final, final)


def _rope_tables(pos):
    half = MLA_ROPE // 2
    inv = ROPE_THETA ** (-jnp.arange(half, dtype=F32) / half)
    ang = pos.astype(F32)[:, None] * inv
    cos, sin = jnp.cos(ang), jnp.sin(ang)
    z = jnp.zeros((pos.shape[0], 128 - MLA_ROPE), F32)
    return jnp.concatenate([cos, cos, z], axis=1), jnp.concatenate([-sin, sin, z], axis=1)


def _swap_halves(w):
    half = w.shape[-1] // 2
    return jnp.concatenate([w[..., half:], w[..., :half]], axis=-1)


def kernel(x_prompt, x_sample, cache_sb_k, cache_sb_v, cache_mla_latent, cache_mla_krope, state_gla, state_hgrn, page_table, c_prompt, c_sample, w_ada, b_ada, g_norm, w_in_ab, g_q_a, w_q_b, g_kv_a, w_kv_b, w_out_ab, w_in_cd, w_gla_gate, b_gla_gate, g_gla_out, hgrn_lb, g_hgrn_out, w_out_cd, w_router, b_router, w_gate_up, b_gate_up, w_down, b_down, g_final):
    batch, seq, d = x_prompt.shape
    db, ds, _ = x_sample.shape
    depth = w_ada.shape[0]
    past_len = page_table.shape[1] * PAGE
    tok = Tok(batch, seq, db, ds)
    assert d == D_MODEL and ds <= DEC_QROWS // SB_GROUP

    x = jnp.concatenate([x_prompt.reshape(tok.n_p, d), x_sample.reshape(tok.n_s, d)], axis=0)
    n_c = batch + db
    c_all = jnp.concatenate([c_prompt, c_sample, jnp.zeros((-n_c % 8, d), F32)], axis=0)
    mod = adaln_call(c_all, w_ada, b_ada)

    pos_all = jnp.concatenate([jnp.tile(jnp.arange(seq, dtype=I32), batch),
                               jnp.tile(past_len + jnp.arange(ds, dtype=I32), db)])
    cos_t, sin_t = _rope_tables(pos_all)
    lb_soft = jax.nn.softmax(hgrn_lb.astype(F32), axis=0)
    lower_bounds = jnp.cumsum(lb_soft, axis=0) - lb_soft[0]

    outs_ab, outs_cd = [], []
    y_final = None
    for l in range(depth):
        j = l // 2
        mod_p = mod[l, :batch].reshape(batch, 1, 6 * d)
        mod_s = jnp.repeat(mod[l, batch:n_c], ds, axis=0)
        h = normmod_call(tok, x, g_norm[l, 0], mod_p, mod_s, 0, 1)
        if l % 2 == 0:
            w_in = w_in_ab[j]
            zc = jnp.zeros((d, 128 - MLA_ROPE), F32)
            w_kr = w_in[:, 2816:2880]
            w_aug = jnp.concatenate([w_in[:, :2816], w_kr, zc, _swap_halves(w_kr), zc], axis=1)
            proj = matmul_call(h, w_aug, 1024, "in_proj_ab")
            wq = w_q_b[j].reshape(MLA_QL, MLA_HEADS, MLA_NOPE + MLA_ROPE)
            zq = jnp.zeros((MLA_QL, MLA_HEADS, 128 - MLA_ROPE), F32)
            wq_rope = wq[:, :, MLA_NOPE:]
            wq_aug = jnp.concatenate([wq[:, :, :MLA_NOPE].reshape(MLA_QL, -1),
                                      jnp.concatenate([wq_rope, zq], axis=2).reshape(MLA_QL, -1),
                                      jnp.concatenate([_swap_halves(wq_rope), zq], axis=2).reshape(MLA_QL, -1)], axis=1)
            wk_abs = jnp.transpose(w_kv_b[j][:, :, :MLA_NOPE], (1, 2, 0))
            wv = jnp.transpose(w_kv_b[j][:, :, MLA_NOPE:], (1, 0, 2))
            sbq, sbk, sbv, sbk_b, sbv_b, lat, krope, qm, km = ab_post_call(proj, cos_t, sin_t, g_q_a[j], g_kv_a[j], wq_aug, wk_abs)
            o_sb_p = sb_prompt_call(tok, sbq, sbk_b, sbv_b)
            o_ml_p = mla_prompt_call(tok, qm, km, wv)
            qsb_d = sbq[tok.n_p:].reshape(db, ds, SB_KV, SB_GROUP, SB_DIM).transpose(0, 2, 3, 1, 4)
            qsb_d = jnp.pad(qsb_d, ((0, 0), (0, 0), (0, 0), (0, DEC_QROWS // SB_GROUP - ds), (0, 0)))
            qsb_d = qsb_d.reshape(db, SB_KV * DEC_QROWS, SB_DIM)
            qml_d = qm[tok.n_p:].reshape(db, ds, MLA_HEADS, MLA_QW).transpose(0, 2, 1, 3)
            qml_d = jnp.pad(qml_d, ((0, 0), (0, 0), (0, DEC_QROWS - ds), (0, 0))).reshape(db, MLA_HEADS * DEC_QROWS, MLA_QW)
            padk = lambda a: jnp.pad(a[tok.n_p:].reshape(db, ds, -1), ((0, 0), (0, PAGE - ds), (0, 0)))
            rows_th = lambda a: padk(a).reshape(db, PAGE * SB_KV, SB_DIM)
            o_sb_s, o_ml_s = decode_call(page_table, qsb_d, qml_d, rows_th(sbk), rows_th(sbv), padk(lat),
                                         jnp.swapaxes(padk(krope), 1, 2), wv,
                                         cache_sb_k, cache_sb_v, cache_mla_latent, jnp.swapaxes(cache_mla_krope, 2, 3), j)
            hs = DEC_QROWS // SB_GROUP
            o_sb_s = o_sb_s.reshape(db, SB_KV, SB_GROUP, hs, SB_DIM)[:, :, :, :ds].transpose(0, 3, 1, 2, 4).reshape(tok.n_s, -1)
            o_ml_s = o_ml_s[:, :ds].reshape(tok.n_s, -1)
            o = jnp.concatenate([jnp.concatenate([o_sb_p, o_ml_p], axis=1), jnp.concatenate([o_sb_s, o_ml_s], axis=1)], axis=0)
            x = matmul_res_call(tok, o, w_out_ab[j], x, mod_p, mod_s, 2, "out_proj_ab")
            outs_ab.append((sbk, sbv, lat, krope))
        else:
            w_in = w_in_cd[j]
            w_main = jnp.concatenate([w_in[:, :2048], w_in[:, 2064:]], axis=1)
            w_ga = jnp.concatenate([w_in[:, 2048:2064], jnp.zeros((d, 128 - GLA_RANK), F32)], axis=1)
            proj = matmul_call(h, w_main, 1024, "in_proj_cd")
            ga = matmul_call(h, w_ga, 128, "in_proj_gate")
            w_gate_pad = jnp.concatenate([w_gla_gate[j], jnp.zeros((128 - GLA_RANK, GLA_HEADS * GLA_DK), F32)], axis=0)
            lb = lower_bounds[l]
            o_gla_p, st_gla_p = scan_prompt_call(tok, proj, ga, w_gate_pad, b_gla_gate[j], g_gla_out[j], lb, "gla")
            o_hgrn_p, st_hgrn_p = scan_prompt_call(tok, proj, ga, w_gate_pad, b_gla_gate[j], g_hgrn_out[j], lb, "hgrn")
            pad_t = lambda a: jnp.pad(a[tok.n_p:].reshape(db, ds, -1), ((0, 0), (0, SCAN_SROWS - ds), (0, 0)))
            proj_s, ga_s = pad_t(proj), pad_t(ga)
            o_gla_s, st_gla_s = scan_sample_call(proj_s, ga_s, w_gate_pad, b_gla_gate[j], g_gla_out[j], lb, state_gla[j], "gla", ds)
            o_hgrn_s, st_hgrn_s = scan_sample_call(proj_s, ga_s, w_gate_pad, b_gla_gate[j], g_hgrn_out[j], lb, state_hgrn[j], "hgrn", ds)
            o_s = jnp.concatenate([o_gla_s[:, :ds].reshape(tok.n_s, -1), o_hgrn_s[:, :ds].reshape(tok.n_s, -1)], axis=1)
            o = jnp.concatenate([jnp.concatenate([o_gla_p, o_hgrn_p], axis=1), o_s], axis=0)
            x = matmul_res_call(tok, o, w_out_cd[j], x, mod_p, mod_s, 2, "out_proj_cd")
            outs_cd.append((st_gla_p, st_hgrn_p, st_gla_s, st_hgrn_s))
        res = moe_layer(tok, l, x, g_norm[l, 1], mod_p, mod_s, w_router, b_router, w_gate_up, b_gate_up,
                        w_down, b_down, g_final, l == depth - 1)
        if l == depth - 1:
            y_final = res
        else:
            x = res[0]

    y_prompt = y_final[0].reshape(batch, seq, d)
    y_sample = y_final[1].reshape(db, ds, d)

    def grp(a, s, shape):
        return a[s].reshape(shape)

    sp, ss = slice(0, tok.n_p), slice(tok.n_p, tok.n)
    sb_k_p = jnp.stack([grp(r[0], sp, (batch, seq, SB_KV, SB_DIM)) for r in outs_ab])
    sb_v_p = jnp.stack([grp(r[1], sp, (batch, seq, SB_KV, SB_DIM)) for r in outs_ab])
    lat_p = jnp.stack([grp(r[2], sp, (batch, seq, MLA_KVL)) for r in outs_ab])
    krope_p = jnp.stack([grp(r[3], sp, (batch, seq, MLA_ROPE)) for r in outs_ab])
    sb_k_s = jnp.stack([grp(r[0], ss, (db, ds, SB_KV, SB_DIM)) for r in outs_ab])
    sb_v_s = jnp.stack([grp(r[1], ss, (db, ds, SB_KV, SB_DIM)) for r in outs_ab])
    lat_s = jnp.stack([grp(r[2], ss, (db, ds, MLA_KVL)) for r in outs_ab])
    krope_s = jnp.stack([grp(r[3], ss, (db, ds, MLA_ROPE)) for r in outs_ab])
    gla_p = jnp.stack([r[0] for r in outs_cd])
    hgrn_p = jnp.stack([r[1] for r in outs_cd])
    gla_s = jnp.stack([r[2] for r in outs_cd])
    hgrn_s = jnp.stack([r[3] for r in outs_cd])
    return (y_prompt, y_sample, sb_k_p, sb_v_p, lat_p, krope_p, gla_p, hgrn_p,
            sb_k_s, sb_v_s, lat_s, krope_s, gla_s, hgrn_s)
```
